```python
import jax
import jax.numpy as jnp
from jax import lax
import numpy as np

D_MODEL = 1024
BATCH = 8
SEQ = 4096
DEPTH = 2

CTX_LEN = 256
GRID_W = 64
HEAD_DIM = 64
Q_BLOCK = 128
ROPE_BASE = 10000.0
A_HEADS = D_MODEL // (2 * HEAD_DIM)
A_KV_HEADS = 2
B_GROUPS = 4
B_GROUP_DIM = D_MODEL // (4 * B_GROUPS)
C_HEADS = D_MODEL // (4 * HEAD_DIM)
C_KV_HEADS = 2
WINDOW = 128
N_EXPERTS = 16
EC_CAPACITY_FACTOR = 2
EXPERT_FF = 2 * D_MODEL

A_Q = A_HEADS * HEAD_DIM
A_KV = A_KV_HEADS * HEAD_DIM
B_W = B_GROUPS * B_GROUP_DIM
C_Q = C_HEADS * HEAD_DIM
C_KV = C_KV_HEADS * HEAD_DIM
MIX_WIDTH = A_Q + B_W + C_Q
QU_WIDTH = A_Q + C_Q + B_W
IN_WIDTH = QU_WIDTH + 2 * A_KV + 2 * C_KV

LN_EPS = 1e-5
RMS_EPS = 1e-6
NEG_INF = -1e30
DEEPNORM_ALPHA = (2 * DEPTH) ** 0.25
DEEPNORM_BETA = (8 * DEPTH) ** -0.25

kernel_name = 'hybrid_gqa_fourier_swa_ecmoe_diffusion'


def _layer_norm(x):
    xf = x.astype(jnp.float32)
    mu = jnp.mean(xf, -1, keepdims=True)
    var = jnp.mean(jnp.square(xf - mu), -1, keepdims=True)
    return ((xf - mu) * lax.rsqrt(var + LN_EPS)).astype(x.dtype)


def _post_norm(res, g, b):
    return _layer_norm(res) * g + b


def _modulate(x, shift, scale):
    return _layer_norm(x) * (1 + scale) + shift


def _rms_norm(x, g):
    xf = x.astype(jnp.float32)
    y = xf * lax.rsqrt(jnp.mean(jnp.square(xf), -1, keepdims=True) + RMS_EPS)
    return y.astype(x.dtype) * g


def _rope_2d_tables(n_tokens, dtype):
    rows = n_tokens // GRID_W
    row = jnp.repeat(jnp.arange(rows, dtype=jnp.int32), GRID_W)
    col = jnp.tile(jnp.arange(GRID_W, dtype=jnp.int32), rows)
    axis_dim = HEAD_DIM // 2
    inv_freq = ROPE_BASE ** (-jnp.arange(0, axis_dim, 2, dtype=jnp.float32) / axis_dim)
    ang = jnp.stack([row[:, None] * inv_freq, col[:, None] * inv_freq], axis=1)
    return jnp.cos(ang).astype(dtype), jnp.sin(ang).astype(dtype)


def _apply_rope_2d(x, cos, sin):
    b, s, h, dh = x.shape
    xr = x.reshape(b, s, h, 2, 2, dh // 4)
    x1, x2 = xr[..., 0, :], xr[..., 1, :]
    cs, sn = cos[None, :, None], sin[None, :, None]
    out = jnp.stack([x1 * cs - x2 * sn, x2 * cs + x1 * sn], axis=-2)
    return out.reshape(b, s, h, dh)


def _heads(p, n):
    b, t, _ = p.shape
    return p.reshape(b, t, n, HEAD_DIM)


def _group_q(q, n_kv):
    b, t, h, dh = q.shape
    return q.reshape(b, t, n_kv, h // n_kv, dh)


def _split_qu(p):
    return p[..., :A_Q], p[..., A_Q:A_Q + C_Q], p[..., A_Q + C_Q:QU_WIDTH]


def _split_kv(p):
    ka = _heads(p[..., :A_KV], A_KV_HEADS)
    va = _heads(p[..., A_KV:2 * A_KV], A_KV_HEADS)
    kc = _heads(p[..., 2 * A_KV:2 * A_KV + C_KV], C_KV_HEADS)
    vc = _heads(p[..., 2 * A_KV + C_KV:], C_KV_HEADS)
    return ka, va, kc, vc


def _dense_block_attention(q, k, v):
    b, s, hkv, g, dh = q.shape
    nb = s // Q_BLOCK
    qb = jnp.moveaxis(q.reshape(b, nb, Q_BLOCK, hkv, g, dh), 1, 0)
    scale = dh ** -0.5

    def one(qblk):
        sc = jnp.einsum('bqhgd,bkhd->bhgqk', qblk, k).astype(jnp.float32) * scale
        p = jax.nn.softmax(sc, axis=-1).astype(v.dtype)
        return jnp.einsum('bhgqk,bkhd->bqhgd', p, v)

    out = lax.map(one, qb)
    return jnp.moveaxis(out, 0, 1).reshape(b, s, hkv * g * dh)


def _sink_softmax(sc, sink):
    b, hkv, g, qn, _ = sc.shape
    sk = jnp.broadcast_to(sink.astype(jnp.float32)[None, :, :, None, None], (b, hkv, g, qn, 1))
    return jax.nn.softmax(jnp.concatenate([sc, sk], axis=-1), axis=-1)[..., :-1]


def _ctx_sink_attention(q, k, v, sink):
    scale = q.shape[-1] ** -0.5
    sc = jnp.einsum('bqhgd,bkhd->bhgqk', q, k).astype(jnp.float32) * scale
    p = _sink_softmax(sc, sink).astype(v.dtype)
    out = jnp.einsum('bhgqk,bkhd->bqhgd', p, v)
    b, t = q.shape[:2]
    return out.reshape(b, t, -1)


def _window_sink_attention(q, k, v, k_ctx, v_ctx, sink):
    b, s, hkv, g, dh = q.shape
    nb = s // Q_BLOCK
    band = Q_BLOCK + 2 * WINDOW
    pad = ((0, 0), (WINDOW, WINDOW), (0, 0), (0, 0))
    kp, vp = jnp.pad(k, pad), jnp.pad(v, pad)
    qb = jnp.moveaxis(q.reshape(b, nb, Q_BLOCK, hkv, g, dh), 1, 0)
    scale = dh ** -0.5
    q_off = jnp.arange(Q_BLOCK)
    k_off = jnp.arange(band) - WINDOW
    in_window = jnp.abs(k_off[None, :] - q_off[:, None]) <= WINDOW
    ctx_valid = jnp.ones((Q_BLOCK, k_ctx.shape[1]), dtype=bool)

    def one(args):
        qblk, i = args
        start = i * Q_BLOCK
        kb = lax.dynamic_slice_in_dim(kp, start, band, axis=1)
        vb = lax.dynamic_slice_in_dim(vp, start, band, axis=1)
        k_pos = start + k_off
        valid = in_window & ((k_pos >= 0) & (k_pos < s))[None, :]
        mask = jnp.concatenate([ctx_valid, valid], axis=-1)
        kk = jnp.concatenate([k_ctx, kb], axis=1)
        vv = jnp.concatenate([v_ctx, vb], axis=1)
        sc = jnp.einsum('bqhgd,bkhd->bhgqk', qblk, kk).astype(jnp.float32) * scale
        sc = jnp.where(mask, sc, NEG_INF)
        p = _sink_softmax(sc, sink).astype(vv.dtype)
        return jnp.einsum('bhgqk,bkhd->bqhgd', p, vv)

    out = lax.map(one, (qb, jnp.arange(nb)))
    return jnp.moveaxis(out, 0, 1).reshape(b, s, hkv * g * dh)


def _fourier_mix(u, w, bias):
    b, t, _ = u.shape
    ug = u.reshape(b, t, B_GROUPS, B_GROUP_DIM).astype(jnp.float32)
    f = jnp.fft.fft2(ug, axes=(1, 3), norm='ortho').real.astype(u.dtype)
    y = jnp.einsum('btgc,gcd->btgd', f, w) + bias
    return y.reshape(b, t, B_W)


def _mixing(h_lat, h_ctx, w_in, q_norm, k_norm, w_four, b_four, sink, w_out, cos, sin, update_ctx):
    sink = sink.reshape(C_KV_HEADS, C_HEADS // C_KV_HEADS)
    ka_c, va_c, kc_c, vc_c = _split_kv(h_ctx @ w_in[:, QU_WIDTH:])
    ka_c = _rms_norm(ka_c, k_norm)
    p = h_lat @ w_in
    qa, qc, u = _split_qu(p)
    ka, va, kc, vc = _split_kv(p[..., QU_WIDTH:])
    qa = _apply_rope_2d(_rms_norm(_heads(qa, A_HEADS), q_norm), cos, sin)
    ka = _apply_rope_2d(_rms_norm(ka, k_norm), cos, sin)
    qc = _apply_rope_2d(_heads(qc, C_HEADS), cos, sin)
    kc = _apply_rope_2d(kc, cos, sin)
    out_a = _dense_block_attention(_group_q(qa, A_KV_HEADS),
                                   jnp.concatenate([ka_c, ka], axis=1),
                                   jnp.concatenate([va_c, va], axis=1))
    out_b = _fourier_mix(u, w_four, b_four)
    out_c = _window_sink_attention(_group_q(qc, C_KV_HEADS), kc, vc, kc_c, vc_c, sink)
    o_lat = jnp.concatenate([out_a, out_b, out_c], axis=-1) @ w_out
    if not update_ctx:
        return o_lat, None
    qa_c, qc_c, u_c = _split_qu(h_ctx @ w_in[:, :QU_WIDTH])
    qa_c = _rms_norm(_heads(qa_c, A_HEADS), q_norm)
    out_a_c = _dense_block_attention(_group_q(qa_c, A_KV_HEADS), ka_c, va_c)
    out_b_c = _fourier_mix(u_c, w_four, b_four)
    out_c_c = _ctx_sink_attention(_group_q(_heads(qc_c, C_HEADS), C_KV_HEADS), kc_c, vc_c, sink)
    o_ctx = jnp.concatenate([out_a_c, out_b_c, out_c_c], axis=-1) @ w_out
    return o_lat, o_ctx


def _expert_choice_moe(h, w_router, w_gate, w_up, w_down):
    b, t, d = h.shape
    capacity = EC_CAPACITY_FACTOR * t // N_EXPERTS
    logits = jnp.einsum('btd,de->bte', h, w_router).astype(jnp.float32)
    affinity = jax.nn.softmax(logits, axis=-1)
    gate, idx = lax.top_k(jnp.swapaxes(affinity, 1, 2), capacity)
    xg = jax.vmap(lambda hb, ib: hb[ib])(h, idx)
    a = jnp.einsum('becd,edf->becf', xg, w_gate)
    up = jnp.einsum('becd,edf->becf', xg, w_up)
    y = jnp.einsum('becf,efd->becd', jax.nn.silu(a) * up, w_down) * gate[..., None].astype(h.dtype)
    return jax.vmap(lambda ib, yb: jnp.zeros((t, d), yb.dtype).at[ib.reshape(-1)].add(yb.reshape(-1, d)))(idx, y)


def setup_inputs(seed: int = 0) -> dict:
    key = jax.random.key(seed)
    ks = jax.random.split(key, 24)
    D = D_MODEL

    def nrm(k, shape, scale):
        return jax.random.normal(k, shape, jnp.float32) * scale

    return {
        'x': nrm(ks[0], (BATCH, SEQ, D), 1.0),
        'c': nrm(ks[1], (BATCH, D), 1.0),
        'ctx': nrm(ks[2], (BATCH, CTX_LEN, D), 1.0),
        'c_ctx': nrm(ks[3], (D,), 1.0),
        'w_mod': nrm(ks[4], (DEPTH, D, 6 * D), 0.5 * D ** -0.5),
        'b_mod': nrm(ks[5], (DEPTH, 6 * D), 0.02),
        'w_in': nrm(ks[6], (DEPTH, D, IN_WIDTH), D ** -0.5),
        'q_norm_a': 1.0 + nrm(ks[7], (DEPTH, HEAD_DIM), 0.02),
        'k_norm_a': 1.0 + nrm(ks[8], (DEPTH, HEAD_DIM), 0.02),
        'w_fourier': nrm(ks[9], (DEPTH, B_GROUPS, B_GROUP_DIM, B_GROUP_DIM), B_GROUP_DIM ** -0.5),
        'b_fourier': nrm(ks[10], (DEPTH, B_GROUPS, B_GROUP_DIM), 0.02),
        'sink_c': nrm(ks[11], (DEPTH, C_HEADS), 0.5),
        'w_out': nrm(ks[12], (DEPTH, MIX_WIDTH, D), DEEPNORM_BETA * MIX_WIDTH ** -0.5),
        'ln1_g': 1.0 + nrm(ks[13], (DEPTH, D), 0.02),
        'ln1_b': nrm(ks[14], (DEPTH, D), 0.02),
        'w_router': nrm(ks[15], (DEPTH, D, N_EXPERTS), D ** -0.5),
        'w_gate': nrm(ks[16], (DEPTH, N_EXPERTS, D, EXPERT_FF), D ** -0.5),
        'w_up': nrm(ks[17], (DEPTH, N_EXPERTS, D, EXPERT_FF), D ** -0.5),
        'w_down': nrm(ks[18], (DEPTH, N_EXPERTS, EXPERT_FF, D), DEEPNORM_BETA * EXPERT_FF ** -0.5),
        'ln2_g': 1.0 + nrm(ks[19], (DEPTH, D), 0.02),
        'ln2_b': nrm(ks[20], (DEPTH, D), 0.02),
    }


def reference(x, c, ctx, c_ctx, w_mod, b_mod, w_in, q_norm_a, k_norm_a, w_fourier, b_fourier, sink_c,
              w_out, ln1_g, ln1_b, w_router, w_gate, w_up, w_down, ln2_g, ln2_b):
    cos, sin = _rope_2d_tables(x.shape[1], x.dtype)
    silu_c = jax.nn.silu(c)
    silu_cc = jax.nn.silu(c_ctx)
    x_lat, x_ctx = x, ctx
    for layer in range(DEPTH):
        update_ctx = layer < DEPTH - 1
        mod_lat = silu_c @ w_mod[layer] + b_mod[layer]
        mod_ctx = silu_cc @ w_mod[layer] + b_mod[layer]
        sh1, sc1, g1, sh2, sc2, g2 = jnp.split(mod_lat[:, None, :], 6, axis=-1)
        csh1, csc1, cg1, csh2, csc2, cg2 = jnp.split(mod_ctx, 6, axis=-1)
        h_lat = _modulate(x_lat, sh1, sc1)
        h_ctx = _modulate(x_ctx, csh1, csc1)
        o_lat, o_ctx = _mixing(h_lat, h_ctx, w_in[layer], q_norm_a[layer], k_norm_a[layer],
                               w_fourier[layer], b_fourier[layer], sink_c[layer], w_out[layer],
                               cos, sin, update_ctx)
        x_lat = _post_norm(DEEPNORM_ALPHA * x_lat + g1 * o_lat, ln1_g[layer], ln1_b[layer])
        y_lat = _expert_choice_moe(_modulate(x_lat, sh2, sc2), w_router[layer], w_gate[layer],
                                   w_up[layer], w_down[layer])
        x_lat = _post_norm(DEEPNORM_ALPHA * x_lat + g2 * y_lat, ln2_g[layer], ln2_b[layer])
        if update_ctx:
            x_ctx = _post_norm(DEEPNORM_ALPHA * x_ctx + cg1 * o_ctx, ln1_g[layer], ln1_b[layer])
            y_ctx = _expert_choice_moe(_modulate(x_ctx, csh2, csc2), w_router[layer], w_gate[layer],
                                       w_up[layer], w_down[layer])
            x_ctx = _post_norm(DEEPNORM_ALPHA * x_ctx + cg2 * y_ctx, ln2_g[layer], ln2_b[layer])
    return x_lat
```

```python
import functools

import numpy as np
import jax
import jax.numpy as jnp
from jax import lax
from jax.experimental import pallas as pl
from jax.experimental.pallas import tpu as pltpu

F32 = jnp.float32
BF16 = jnp.bfloat16

D_MODEL = 1024
DEPTH = 2
GRID_W = 64
HEAD_DIM = 64
ROPE_BASE = 10000.0
A_HEADS, A_KV_HEADS = 8, 2
B_GROUPS, B_GROUP_DIM = 4, 64
C_HEADS, C_KV_HEADS = 4, 2
WINDOW = 128
N_EXPERTS = 16
EC_CAPACITY_FACTOR = 2
EXPERT_FF = 2 * D_MODEL
A_Q, A_KV, B_W, C_Q, C_KV = 512, 128, 256, 256, 128
QU_WIDTH = A_Q + C_Q + B_W
IN_WIDTH = QU_WIDTH + 2 * A_KV + 2 * C_KV
LN_EPS = 1e-5
RMS_EPS = 1e-6
NEG_INF = -1e30
DEEPNORM_ALPHA = (2 * DEPTH) ** 0.25
MOD_ROWS = 16
LANES = 128
VMEM_LIMIT = 56 * 1024 * 1024

HIGHEST = lax.Precision.HIGHEST
NT_DIMS = (((1,), (1,)), ((), ()))


def _params(sem):
    return pltpu.CompilerParams(dimension_semantics=sem, vmem_limit_bytes=VMEM_LIMIT)


def _layer_norm(x):
    mu = jnp.mean(x, axis=-1, keepdims=True)
    xc = x - mu
    var = jnp.mean(xc * xc, axis=-1, keepdims=True)
    return xc * lax.rsqrt(var + LN_EPS)


def _silu(x):
    return x * (1.0 / (1.0 + jnp.exp(-x)))


def _mod_body(c_ref, w_ref, b_ref, o_ref):
    s = _silu(c_ref[...])
    o_ref[0] = jnp.dot(s, w_ref[0], precision=HIGHEST, preferred_element_type=F32) + b_ref[0]


def _modulation(cc, w_mod, b_mod):
    tn = 1024
    n = w_mod.shape[-1]
    return pl.pallas_call(
        _mod_body,
        grid=(DEPTH, n // tn),
        in_specs=[
            pl.BlockSpec((MOD_ROWS, D_MODEL), lambda l, j: (0, 0)),
            pl.BlockSpec((1, D_MODEL, tn), lambda l, j: (l, 0, j)),
            pl.BlockSpec((1, 1, tn), lambda l, j: (l, 0, j)),
        ],
        out_specs=pl.BlockSpec((1, MOD_ROWS, tn), lambda l, j: (l, 0, j)),
        out_shape=jax.ShapeDtypeStruct((DEPTH, MOD_ROWS, n), F32),
        compiler_params=_params(("arbitrary", "arbitrary")),
        name="modulation",
    )(cc, w_mod, b_mod.reshape(DEPTH, 1, n))


def _rope(x, cos, sin_signed, first_half):
    outs = []
    for j in range(x.shape[1] // LANES):
        xj = x[:, LANES * j:LANES * (j + 1)]
        partner = jnp.where(first_half, pltpu.roll(xj, LANES - 16, 1), pltpu.roll(xj, 16, 1))
        outs.append(xj * cos + partner * sin_signed)
    return jnp.concatenate(outs, axis=1) if len(outs) > 1 else outs[0]


def _dup_kv_heads(k, copies):
    lane = lax.broadcasted_iota(jnp.int32, (1, LANES), 1)
    lo = lane < HEAD_DIM
    r = pltpu.roll(k, HEAD_DIM, 1)
    d0 = jnp.where(lo, k, r)
    d1 = jnp.where(lo, r, k)
    reps = copies // 2
    return jnp.concatenate([d0] * reps + [d1] * reps, axis=1)


def _inproj_body(x_ref, sh_ref, sc_ref, w_ref, qn_ref, kn_ref, gq_ref, gk_ref, cos_ref, sin_ref,
                 qa_ref, qc_ref, u_ref, ka_ref, va_ref, kc_ref, vc_ref, *, rope):
    h = _layer_norm(x_ref[0]) * (1.0 + sc_ref[0]) + sh_ref[0]
    p = jnp.dot(h.astype(BF16), w_ref[...], preferred_element_type=F32)
    qa = p[:, :A_Q]
    qc = p[:, A_Q:A_Q + C_Q]
    u = p[:, A_Q + C_Q:QU_WIDTH]
    ka = p[:, QU_WIDTH:QU_WIDTH + A_KV]
    va = p[:, QU_WIDTH + A_KV:QU_WIDTH + 2 * A_KV]
    kc = p[:, QU_WIDTH + 2 * A_KV:QU_WIDTH + 2 * A_KV + C_KV]
    vc = p[:, QU_WIDTH + 2 * A_KV + C_KV:]
    msq = jnp.dot((qa * qa).astype(BF16), gq_ref[...], preferred_element_type=F32) * (1.0 / HEAD_DIM)
    qa = qa * lax.rsqrt(msq + RMS_EPS) * qn_ref[...]
    msk = jnp.dot((ka * ka).astype(BF16), gk_ref[...], preferred_element_type=F32) * (1.0 / HEAD_DIM)
    ka = ka * lax.rsqrt(msk + RMS_EPS) * kn_ref[...]
    if rope:
        cos = cos_ref[...]
        sin = sin_ref[...]
        lane = lax.broadcasted_iota(jnp.int32, (1, LANES), 1)
        first_half = (lane % 32) < 16
        qa = _rope(qa, cos, sin, first_half)
        qc = _rope(qc, cos, sin, first_half)
        ka = _rope(ka, cos, sin, first_half)
        kc = _rope(kc, cos, sin, first_half)
    scale = HEAD_DIM ** -0.5
    qa_ref[0] = (qa * scale).astype(BF16)
    qc_ref[0] = (qc * scale).astype(BF16)
    u_ref[0] = u
    ka_ref[0] = _dup_kv_heads(ka, A_HEADS // A_KV_HEADS).astype(BF16)
    va_ref[0] = _dup_kv_heads(va, A_HEADS // A_KV_HEADS).astype(BF16)
    kc_ref[0] = _dup_kv_heads(kc, C_HEADS // C_KV_HEADS).astype(BF16)
    vc_ref[0] = _dup_kv_heads(vc, C_HEADS // C_KV_HEADS).astype(BF16)


def _in_projection(x, sh, sc, w_bf16, qn, kn, gq, gk, cos, sin, *, rope):
    b, t, d = x.shape
    tm = min(t, 512)
    row = lambda i, j: (i, j, 0)
    per_b = lambda i, j: (i, 0, 0)
    const = lambda i, j: (0, 0)
    tab = (lambda i, j: (j, 0)) if rope else const
    outs = [(A_Q, BF16), (C_Q, BF16), (B_W, F32), (4 * A_KV, BF16), (4 * A_KV, BF16),
            (2 * C_KV, BF16), (2 * C_KV, BF16)]
    return pl.pallas_call(
        functools.partial(_inproj_body, rope=rope),
        grid=(b, t // tm),
        in_specs=[
            pl.BlockSpec((1, tm, d), row),
            pl.BlockSpec((1, 1, d), per_b),
            pl.BlockSpec((1, 1, d), per_b),
            pl.BlockSpec((d, IN_WIDTH), const),
            pl.BlockSpec((1, A_Q), const),
            pl.BlockSpec((1, A_KV), const),
            pl.BlockSpec((A_Q, A_Q), const),
            pl.BlockSpec((A_KV, A_KV), const),
            pl.BlockSpec((tm, LANES), tab),
            pl.BlockSpec((tm, LANES), tab),
        ],
        out_specs=[pl.BlockSpec((1, tm, w), row) for w, _ in outs],
        out_shape=[jax.ShapeDtypeStruct((b, t, w), dt) for w, dt in outs],
        compiler_params=_params(("arbitrary", "arbitrary")),
        name="in_projection",
    )(x, sh, sc, w_bf16, qn, kn, gq, gk, cos, sin)


def _attn_body(*refs, heads_per_kv, mode, has_sink, tq, t_lat, kc_full):
    refs = list(refs)
    q_ref, kctx_ref, vctx_ref = refs[:3]
    pos = 3
    klat_ref = vlat_ref = sink_ref = None
    if mode != "none":
        klat_ref, vlat_ref = refs[pos:pos + 2]
        pos += 2
    if has_sink:
        sink_ref = refs[pos]
        pos += 1
    o_ref = refs[pos]

    width = heads_per_kv * HEAD_DIM
    kvh = pl.program_id(1)
    qi = pl.program_id(2)
    q = q_ref[0]
    lane = lax.broadcasted_iota(jnp.int32, (1, width), 1)
    total = jnp.zeros((tq, width), F32)

    for g in range(heads_per_kv):
        head_lanes = (lane // HEAD_DIM) == g
        qg = jnp.where(head_lanes, q, jnp.zeros_like(q))

        def step(carry, k, v, mask=None):
            m, l, acc = carry
            s = lax.dot_general(qg, k, NT_DIMS, preferred_element_type=F32)
            if mask is not None:
                s = jnp.where(mask, s, NEG_INF)
            m_new = jnp.maximum(m, jnp.max(s, axis=1, keepdims=True))
            alpha = jnp.exp(m - m_new)
            p = jnp.exp(s - m_new)
            l = alpha * l + jnp.sum(p, axis=1, keepdims=True)
            vg = jnp.where(head_lanes, v, jnp.zeros_like(v))
            acc = acc * alpha + jnp.dot(p.astype(BF16), vg, preferred_element_type=F32)
            return m_new, l, acc

        if has_sink:
            m0 = jnp.full((tq, 1), sink_ref[kvh * heads_per_kv + g], F32)
            l0 = jnp.ones((tq, 1), F32)
        else:
            m0 = jnp.full((tq, 1), NEG_INF, F32)
            l0 = jnp.zeros((tq, 1), F32)
        carry = (m0, l0, jnp.zeros((tq, width), F32))
        carry = step(carry, kctx_ref[0], vctx_ref[0])
        if mode == "full":
            def chunk(c, carry):
                start = pl.multiple_of(c * kc_full, kc_full)
                return step(carry, klat_ref[0, pl.ds(start, kc_full), :], vlat_ref[0, pl.ds(start, kc_full), :])
            carry = lax.fori_loop(0, t_lat // kc_full, chunk, carry)
        elif mode == "window":
            span = tq + 2 * WINDOW
            q0 = qi * tq
            k0 = pl.multiple_of(jnp.clip(q0 - WINDOW, 0, t_lat - span), WINDOW)
            qpos = q0 + lax.broadcasted_iota(jnp.int32, (tq, 1), 0)
            kpos = k0 + lax.broadcasted_iota(jnp.int32, (1, span), 1)
            in_window = jnp.abs(kpos - qpos) <= WINDOW
            carry = step(carry, klat_ref[0, pl.ds(k0, span), :], vlat_ref[0, pl.ds(k0, span), :], in_window)
        _, l, acc = carry
        total = total + acc * (1.0 / l)
    o_ref[0] = total.astype(o_ref.dtype)


def _attention(q, kctx, vctx, klat=None, vlat=None, sink=None, *, heads_per_kv, mode):
    b, t, _ = q.shape
    n_kv = 2
    width = heads_per_kv * HEAD_DIM
    t_ctx = kctx.shape[1]
    tq = min(t, 256)
    qmap = lambda i, h, j: (i, j, h)
    kmap = lambda i, h, j: (i, 0, h)
    in_specs = [pl.BlockSpec((1, tq, width), qmap),
                pl.BlockSpec((1, t_ctx, width), kmap),
                pl.BlockSpec((1, t_ctx, width), kmap)]
    args = [q, kctx, vctx]
    t_lat = 0
    if mode != "none":
        t_lat = klat.shape[1]
        in_specs += [pl.BlockSpec((1, t_lat, width), kmap), pl.BlockSpec((1, t_lat, width), kmap)]
        args += [klat, vlat]
    if sink is not None:
        in_specs.append(pl.BlockSpec(memory_space=pltpu.SMEM))
        args.append(sink)
    body = functools.partial(_attn_body, heads_per_kv=heads_per_kv, mode=mode, has_sink=sink is not None,
                             tq=tq, t_lat=t_lat, kc_full=min(max(t_lat, 1), 1024))
    return pl.pallas_call(
        body,
        grid=(b, n_kv, t // tq),
        in_specs=in_specs,
        out_specs=pl.BlockSpec((1, tq, width), qmap),
        out_shape=jax.ShapeDtypeStruct((b, t, n_kv * width), BF16),
        compiler_params=_params(("arbitrary", "arbitrary", "arbitrary")),
        name=f"attention_{mode}_g{heads_per_kv}",
    )(*args)


def _four_chan_body(u_ref, wblk_ref, cc_ref, sc_ref, z_ref):
    w = wblk_ref[...]
    wc = jnp.dot(cc_ref[...], w, precision=HIGHEST, preferred_element_type=F32)
    ws = jnp.dot(sc_ref[...], w, precision=HIGHEST, preferred_element_type=F32)
    u = u_ref[0]
    z_ref[0, 0] = jnp.dot(u, wc, precision=HIGHEST, preferred_element_type=F32)
    z_ref[0, 1] = -jnp.dot(u, ws, precision=HIGHEST, preferred_element_type=F32)


def _four_rows_body(z_ref, m_ref, tc_ref, ts_ref, a_ref):
    n1 = z_ref.shape[2]
    z = jnp.concatenate([z_ref[0, 0], z_ref[0, 1]], axis=0)
    a = jnp.dot(m_ref[...], z, precision=HIGHEST, preferred_element_type=F32)
    ar, ai = a[:n1], a[n1:]
    tc, ts = tc_ref[...], ts_ref[...]
    a_ref[0, 0] = ar * tc + ai * ts
    a_ref[0, 1] = ai * tc - ar * ts


def _four_cols_body(a_ref, cs_ref, bias_ref, y_ref):
    kb = a_ref.shape[2]
    for j in range(kb):
        rhs = jnp.concatenate([a_ref[0, 0, j], a_ref[0, 1, j]], axis=0)
        res = jnp.dot(cs_ref[...], rhs, precision=HIGHEST, preferred_element_type=F32)
        y_ref[0, :, j, :] = res + bias_ref[...]


def _dft_constants(t):
    n = int(round(t ** 0.5))
    assert n * n == t
    k = np.arange(n)
    ang = 2.0 * np.pi * np.outer(k, k) / n
    c, s = np.cos(ang), np.sin(ang)
    rows = np.block([[c, s], [-s, c]]).astype(np.float32)
    cols = np.concatenate([c, s], axis=1).astype(np.float32)
    tw = 2.0 * np.pi * np.outer(k, k) / t
    tc = np.repeat(np.cos(tw), B_W, axis=1).astype(np.float32)
    ts = np.repeat(np.sin(tw), B_W, axis=1).astype(np.float32)
    return n, rows, cols, tc, ts


def _channel_constants(t):
    k = np.arange(B_GROUP_DIM)
    ang = 2.0 * np.pi * np.outer(k, k) / B_GROUP_DIM
    norm = 1.0 / np.sqrt(float(t) * B_GROUP_DIM)
    eye = np.eye(B_GROUPS)
    cc = np.kron(eye, np.cos(ang) * norm).astype(np.float32)
    sc = np.kron(eye, np.sin(ang) * norm).astype(np.float32)
    return cc, sc


def _fourier_mix(u, w_blockdiag, bias):
    b, t, _ = u.shape
    n, rows, cols, tc, ts = _dft_constants(t)
    cc, sc = _channel_constants(t)
    tm = min(t, 1024)
    z = pl.pallas_call(
        _four_chan_body,
        grid=(b, t // tm),
        in_specs=[pl.BlockSpec((1, tm, B_W), lambda i, j: (i, j, 0)),
                  pl.BlockSpec((B_W, B_W), lambda i, j: (0, 0)),
                  pl.BlockSpec((B_W, B_W), lambda i, j: (0, 0)),
                  pl.BlockSpec((B_W, B_W), lambda i, j: (0, 0))],
        out_specs=pl.BlockSpec((1, 2, tm, B_W), lambda i, j: (i, 0, j, 0)),
        out_shape=jax.ShapeDtypeStruct((b, 2, t, B_W), F32),
        compiler_params=_params(("arbitrary", "arbitrary")),
        name="fourier_channels",
    )(u, w_blockdiag, jnp.asarray(cc), jnp.asarray(sc))
    cols_total = n * B_W
    tc_cols = min(cols_total, 4096)
    a = pl.pallas_call(
        _four_rows_body,
        grid=(b, cols_total // tc_cols),
        in_specs=[pl.BlockSpec((1, 2, n, tc_cols), lambda i, j: (i, 0, 0, j)),
                  pl.BlockSpec((2 * n, 2 * n), lambda i, j: (0, 0)),
                  pl.BlockSpec((n, tc_cols), lambda i, j: (0, j)),
                  pl.BlockSpec((n, tc_cols), lambda i, j: (0, j))],
        out_specs=pl.BlockSpec((1, 2, n, tc_cols), lambda i, j: (i, 0, 0, j)),
        out_shape=jax.ShapeDtypeStruct((b, 2, n, cols_total), F32),
        compiler_params=_params(("arbitrary", "arbitrary")),
        name="fourier_rows",
    )(z.reshape(b, 2, n, cols_total), jnp.asarray(rows), jnp.asarray(tc), jnp.asarray(ts))
    kb = 8
    y = pl.pallas_call(
        _four_cols_body,
        grid=(b, n // kb),
        in_specs=[pl.BlockSpec((1, 2, kb, n, B_W), lambda i, j: (i, 0, j, 0, 0)),
                  pl.BlockSpec((n, 2 * n), lambda i, j: (0, 0)),
                  pl.BlockSpec((1, B_W), lambda i, j: (0, 0))],
        out_specs=pl.BlockSpec((1, n, kb, B_W), lambda i, j: (i, 0, j, 0)),
        out_shape=jax.ShapeDtypeStruct((b, n, n, B_W), F32),
        compiler_params=_params(("arbitrary", "arbitrary")),
        name="fourier_cols",
    )(a.reshape(b, 2, n, n, B_W), jnp.asarray(cols), bias)
    return y.reshape(b, t, B_W)


def _outproj_body(a_ref, f_ref, c_ref, w_ref, x_ref, g1_ref, sh2_ref, sc2_ref, lng_ref, lnb_ref, wr_ref,
                  x1_ref, h2_ref, aff_ref):
    o = jnp.dot(a_ref[0], w_ref[:A_Q], preferred_element_type=F32)
    o = o + jnp.dot(f_ref[0].astype(BF16), w_ref[A_Q:A_Q + B_W], preferred_element_type=F32)
    o = o + jnp.dot(c_ref[0], w_ref[A_Q + B_W:], preferred_element_type=F32)
    x1 = _layer_norm(DEEPNORM_ALPHA * x_ref[0] + g1_ref[0] * o) * lng_ref[...] + lnb_ref[...]
    x1_ref[0] = x1
    h2 = _layer_norm(x1) * (1.0 + sc2_ref[0]) + sh2_ref[0]
    h2_ref[0] = h2
    logits = lax.dot_general(wr_ref[...], h2, NT_DIMS, precision=HIGHEST, preferred_element_type=F32)
    e = jnp.exp(logits - jnp.max(logits, axis=0, keepdims=True))
    aff_ref[0] = e / jnp.sum(e, axis=0, keepdims=True)


def _out_projection(oa, of, oc, w_bf16, x, g1, sh2, sc2, lng, lnb, wr_t):
    b, t, d = x.shape
    tm = min(t, 512)
    row = lambda i, j: (i, j, 0)
    per_b = lambda i, j: (i, 0, 0)
    const = lambda i, j: (0, 0)
    return pl.pallas_call(
        _outproj_body,
        grid=(b, t // tm),
        in_specs=[pl.BlockSpec((1, tm, A_Q), row),
                  pl.BlockSpec((1, tm, B_W), row),
                  pl.BlockSpec((1, tm, C_Q), row),
                  pl.BlockSpec((d, d), const),
                  pl.BlockSpec((1, tm, d), row),
                  pl.BlockSpec((1, 1, d), per_b),
                  pl.BlockSpec((1, 1, d), per_b),
                  pl.BlockSpec((1, 1, d), per_b),
                  pl.BlockSpec((1, d), const),
                  pl.BlockSpec((1, d), const),
                  pl.BlockSpec((N_EXPERTS, d), const)],
        out_specs=[pl.BlockSpec((1, tm, d), row),
                   pl.BlockSpec((1, tm, d), row),
                   pl.BlockSpec((1, N_EXPERTS, tm), lambda i, j: (i, 0, j))],
        out_shape=[jax.ShapeDtypeStruct((b, t, d), F32),
                   jax.ShapeDtypeStruct((b, t, d), F32),
                   jax.ShapeDtypeStruct((b, N_EXPERTS, t), F32)],
        compiler_params=_params(("arbitrary", "arbitrary")),
        name="out_projection",
    )(oa, of, oc, w_bf16, x, g1, sh2, sc2, lng, lnb, wr_t)


def _ffn_body(x_ref, wg_ref, wu_ref, wd_ref, gate_ref, o_ref):
    f = pl.program_id(2)
    x = x_ref[0]
    a = jnp.dot(x, wg_ref[0].astype(BF16), preferred_element_type=F32)
    up = jnp.dot(x, wu_ref[0].astype(BF16), preferred_element_type=F32)
    h = (_silu(a) * up).astype(BF16)
    y = jnp.dot(h, wd_ref[0].astype(BF16), preferred_element_type=F32)

    @pl.when(f == 0)
    def _():
        o_ref[0] = y

    @pl.when(f > 0)
    def _():
        o_ref[0] += y

    @pl.when(f == pl.num_programs(2) - 1)
    def _():
        o_ref[0] *= gate_ref[0]


def _expert_ffn(xg, w_gate, w_up, w_down, gate):
    e, m, d = xg.shape
    ff = w_gate.shape[-1]
    tm = min(m, 2048)
    tf = 512
    return pl.pallas_call(
        _ffn_body,
        grid=(e, m // tm, ff // tf),
        in_specs=[pl.BlockSpec((1, tm, d), lambda i, j, k: (i, j, 0)),
                  pl.BlockSpec((1, d, tf), lambda i, j, k: (i, 0, k)),
                  pl.BlockSpec((1, d, tf), lambda i, j, k: (i, 0, k)),
                  pl.BlockSpec((1, tf, d), lambda i, j, k: (i, k, 0)),
                  pl.BlockSpec((1, tm, 1), lambda i, j, k: (i, j, 0))],
        out_specs=pl.BlockSpec((1, tm, d), lambda i, j, k: (i, j, 0)),
        out_shape=jax.ShapeDtypeStruct((e, m, d), F32),
        compiler_params=_params(("arbitrary", "arbitrary", "arbitrary")),
        name="expert_ffn",
    )(xg, w_gate, w_up, w_down, gate)


def _postnorm_body(x_ref, y_ref, g_ref, lng_ref, lnb_ref, o_ref):
    o_ref[0] = _layer_norm(DEEPNORM_ALPHA * x_ref[0] + g_ref[0] * y_ref[0]) * lng_ref[...] + lnb_ref[...]


def _post_norm(x, y, g, lng, lnb):
    b, t, d = x.shape
    tm = min(t, 1024)
    row = lambda i, j: (i, j, 0)
    return pl.pallas_call(
        _postnorm_body,
        grid=(b, t // tm),
        in_specs=[pl.BlockSpec((1, tm, d), row), pl.BlockSpec((1, tm, d), row),
                  pl.BlockSpec((1, 1, d), lambda i, j: (i, 0, 0)),
                  pl.BlockSpec((1, d), lambda i, j: (0, 0)), pl.BlockSpec((1, d), lambda i, j: (0, 0))],
        out_specs=pl.BlockSpec((1, tm, d), row),
        out_shape=jax.ShapeDtypeStruct((b, t, d), F32),
        compiler_params=_params(("arbitrary", "arbitrary")),
        name="post_norm",
    )(x, y, g, lng, lnb)


def _rope_tables(t):
    tok = jnp.arange(t, dtype=jnp.int32)
    lane = np.arange(LANES)
    d = lane % HEAD_DIM
    inv_freq = ROPE_BASE ** (-jnp.arange(0, HEAD_DIM // 2, 2, dtype=F32) / (HEAD_DIM // 2))
    freq = inv_freq[d % 16]
    use_col = jnp.asarray((d // 32) == 1)
    position = jnp.where(use_col[None, :], (tok % GRID_W)[:, None], (tok // GRID_W)[:, None])
    ang = position * freq[None, :]
    sign = jnp.asarray(np.where((d % 32) < 16, -1.0, 1.0).astype(np.float32))
    return jnp.cos(ang).astype(F32), (jnp.sin(ang) * sign[None, :]).astype(F32)


def _block_ones(width):
    idx = np.arange(width) // HEAD_DIM
    return jnp.asarray((idx[:, None] == idx[None, :]).astype(np.float32), dtype=BF16)


def _moe(h2, aff, w_gate, w_up, w_down):
    b, t, d = h2.shape
    cap = EC_CAPACITY_FACTOR * t // N_EXPERTS
    gate, idx = lax.top_k(aff, cap)
    xg = jax.vmap(lambda hb, ib: hb[ib])(h2, idx).astype(BF16)
    xg = jnp.swapaxes(xg, 0, 1).reshape(N_EXPERTS, b * cap, d)
    gate_e = jnp.swapaxes(gate, 0, 1).reshape(N_EXPERTS, b * cap, 1)
    y = _expert_ffn(xg, w_gate, w_up, w_down, gate_e)
    y = jnp.swapaxes(y.reshape(N_EXPERTS, b, cap, d), 0, 1)
    return jax.vmap(lambda ib, yb: jnp.zeros((t, d), F32).at[ib.reshape(-1)].add(yb.reshape(-1, d)))(idx, y)


def kernel(x, c, ctx, c_ctx, w_mod, b_mod, w_in, q_norm_a, k_norm_a, w_fourier, b_fourier, sink_c, w_out, ln1_g,
           ln1_b, w_router, w_gate, w_up, w_down, ln2_g, ln2_b):
    b, t, d = x.shape
    cos, sin = _rope_tables(t)
    gq, gk = _block_ones(A_Q), _block_ones(A_KV)
    cc = jnp.zeros((MOD_ROWS, d), F32).at[:b].set(c).at[b].set(c_ctx)
    mod = _modulation(cc, w_mod, b_mod)
    w_in_bf = w_in.astype(BF16)
    w_out_bf = w_out.astype(BF16)

    x_lat, x_ctx = x, ctx
    for layer in range(DEPTH):
        update_ctx = layer < DEPTH - 1
        lat = [mod[layer, :b, i * d:(i + 1) * d][:, None, :] for i in range(6)]
        cm = [jnp.broadcast_to(mod[layer, b, i * d:(i + 1) * d][None, None, :], (b, 1, d)) for i in range(6)]
        qn = jnp.tile(q_norm_a[layer], A_HEADS)[None, :]
        kn = jnp.tile(k_norm_a[layer], A_KV_HEADS)[None, :]
        w_four = jax.scipy.linalg.block_diag(*[w_fourier[layer, g] for g in range(B_GROUPS)])
        bias = b_fourier[layer].reshape(1, B_W)
        lng1, lnb1 = ln1_g[layer][None, :], ln1_b[layer][None, :]
        lng2, lnb2 = ln2_g[layer][None, :], ln2_b[layer][None, :]
        wr_t = w_router[layer].T
        sink = sink_c[layer]

        proj_ctx = _in_projection(x_ctx, cm[0], cm[1], w_in_bf[layer], qn, kn, gq, gk, cos, sin, rope=False)
        qa_c, qc_c, u_c, ka_c, va_c, kc_c, vc_c = proj_ctx
        qa, qc, u, ka, va, kc, vc = _in_projection(x_lat, lat[0], lat[1], w_in_bf[layer], qn, kn, gq, gk, cos, sin,
                                                   rope=True)
        out_a = _attention(qa, ka_c, va_c, ka, va, heads_per_kv=4, mode="full")
        out_b = _fourier_mix(u, w_four, bias)
        out_c = _attention(qc, kc_c, vc_c, kc, vc, sink, heads_per_kv=2, mode="window")
        x1, h2, aff = _out_projection(out_a, out_b, out_c, w_out_bf[layer], x_lat, lat[2], lat[3], lat[4],
                                      lng1, lnb1, wr_t)
        y = _moe(h2, aff, w_gate[layer], w_up[layer], w_down[layer])
        x_lat = _post_norm(x1, y, lat[5], lng2, lnb2)

        if update_ctx:
            out_a_c = _attention(qa_c, ka_c, va_c, heads_per_kv=4, mode="none")
            out_b_c = _fourier_mix(u_c, w_four, bias)
            out_c_c = _attention(qc_c, kc_c, vc_c, sink=sink, heads_per_kv=2, mode="none")
            x1c, h2c, affc = _out_projection(out_a_c, out_b_c, out_c_c, w_out_bf[layer], x_ctx, cm[2], cm[3], cm[4],
                                             lng1, lnb1, wr_t)
            yc = _moe(h2c, affc, w_gate[layer], w_up[layer], w_down[layer])
            x_ctx = _post_norm(x1c, yc, cm[5], lng2, lnb2)
    return x_lat
```

```python
import functools

import numpy as np
import jax
import jax.numpy as jnp
from jax import lax
from jax.experimental import pallas as pl
from jax.experimental.pallas import tpu as pltpu

F32 = jnp.float32
BF16 = jnp.bfloat16

D_MODEL = 1024
DEPTH = 2
GRID_W = 64
HEAD_DIM = 64
ROPE_BASE = 10000.0
A_HEADS, A_KV_HEADS = 8, 2
B_GROUPS, B_GROUP_DIM = 4, 64
C_HEADS, C_KV_HEADS = 4, 2
WINDOW = 128
N_EXPERTS = 16
EC_CAPACITY_FACTOR = 2
EXPERT_FF = 2 * D_MODEL
A_Q, A_KV, B_W, C_Q, C_KV = 512, 128, 256, 256, 128
QU_WIDTH = A_Q + C_Q + B_W
IN_WIDTH = QU_WIDTH + 2 * A_KV + 2 * C_KV
LN_EPS = 1e-5
RMS_EPS = 1e-6
NEG_INF = -1e30
LOG2E = 1.4426950408889634
SAFE_LOGIT = 60.0
DEEPNORM_ALPHA = (2 * DEPTH) ** 0.25
MOD_ROWS = 16
LANES = 128
VMEM_LIMIT = 56 * 1024 * 1024

HIGHEST = lax.Precision.HIGHEST
NT_DIMS = (((1,), (1,)), ((), ()))


def _params(sem):
    return pltpu.CompilerParams(dimension_semantics=sem, vmem_limit_bytes=VMEM_LIMIT)


def _layer_norm(x):
    mu = jnp.mean(x, axis=-1, keepdims=True)
    xc = x - mu
    var = jnp.mean(xc * xc, axis=-1, keepdims=True)
    return xc * lax.rsqrt(var + LN_EPS)


def _silu(x):
    return x * (1.0 / (1.0 + jnp.exp(-x)))


def _mod_body(c_ref, w_ref, b_ref, o_ref):
    s = _silu(c_ref[...])
    o_ref[0] = jnp.dot(s, w_ref[0], precision=HIGHEST, preferred_element_type=F32) + b_ref[0]


def _modulation(cc, w_mod, b_mod):
    tn = 1024
    n = w_mod.shape[-1]
    return pl.pallas_call(
        _mod_body,
        grid=(DEPTH, n // tn),
        in_specs=[
            pl.BlockSpec((MOD_ROWS, D_MODEL), lambda l, j: (0, 0)),
            pl.BlockSpec((1, D_MODEL, tn), lambda l, j: (l, 0, j)),
            pl.BlockSpec((1, 1, tn), lambda l, j: (l, 0, j)),
        ],
        out_specs=pl.BlockSpec((1, MOD_ROWS, tn), lambda l, j: (l, 0, j)),
        out_shape=jax.ShapeDtypeStruct((DEPTH, MOD_ROWS, n), F32),
        compiler_params=_params(("arbitrary", "arbitrary")),
        name="modulation",
    )(cc, w_mod, b_mod.reshape(DEPTH, 1, n))


def _rope(x, cos, sin_signed, first_half):
    outs = []
    for j in range(x.shape[1] // LANES):
        xj = x[:, LANES * j:LANES * (j + 1)]
        partner = jnp.where(first_half, pltpu.roll(xj, LANES - 16, 1), pltpu.roll(xj, 16, 1))
        outs.append(xj * cos + partner * sin_signed)
    return jnp.concatenate(outs, axis=1) if len(outs) > 1 else outs[0]


def _dup_kv_heads(k, copies):
    lane = lax.broadcasted_iota(jnp.int32, (1, LANES), 1)
    lo = lane < HEAD_DIM
    r = pltpu.roll(k, HEAD_DIM, 1)
    d0 = jnp.where(lo, k, r)
    d1 = jnp.where(lo, r, k)
    reps = copies // 2
    return jnp.concatenate([d0] * reps + [d1] * reps, axis=1)


def _inproj_body(x_ref, sh_ref, sc_ref, w_ref, qn_ref, kn_ref, gq_ref, gk_ref, cos_ref, sin_ref,
                 qa_ref, qc_ref, u_ref, ka_ref, va_ref, kc_ref, vc_ref, *, rope):
    h = _layer_norm(x_ref[0]) * (1.0 + sc_ref[0]) + sh_ref[0]
    p = jnp.dot(h.astype(BF16), w_ref[...], preferred_element_type=F32)
    qa = p[:, :A_Q]
    qc = p[:, A_Q:A_Q + C_Q]
    u = p[:, A_Q + C_Q:QU_WIDTH]
    ka = p[:, QU_WIDTH:QU_WIDTH + A_KV]
    va = p[:, QU_WIDTH + A_KV:QU_WIDTH + 2 * A_KV]
    kc = p[:, QU_WIDTH + 2 * A_KV:QU_WIDTH + 2 * A_KV + C_KV]
    vc = p[:, QU_WIDTH + 2 * A_KV + C_KV:]
    msq = jnp.dot((qa * qa).astype(BF16), gq_ref[...], preferred_element_type=F32) * (1.0 / HEAD_DIM)
    qa = qa * lax.rsqrt(msq + RMS_EPS) * qn_ref[...]
    msk = jnp.dot((ka * ka).astype(BF16), gk_ref[...], preferred_element_type=F32) * (1.0 / HEAD_DIM)
    ka = ka * lax.rsqrt(msk + RMS_EPS) * kn_ref[...]
    if rope:
        cos = cos_ref[...]
        sin = sin_ref[...]
        lane = lax.broadcasted_iota(jnp.int32, (1, LANES), 1)
        first_half = (lane % 32) < 16
        qa = _rope(qa, cos, sin, first_half)
        qc = _rope(qc, cos, sin, first_half)
        ka = _rope(ka, cos, sin, first_half)
        kc = _rope(kc, cos, sin, first_half)
    scale = HEAD_DIM ** -0.5 * LOG2E
    qa_ref[0] = (qa * scale).astype(BF16)
    qc_ref[0] = (qc * scale).astype(BF16)
    u_ref[0] = u
    ka_ref[0] = _dup_kv_heads(ka, A_HEADS // A_KV_HEADS).astype(BF16)
    va_ref[0] = _dup_kv_heads(va, A_HEADS // A_KV_HEADS).astype(BF16)
    kc_ref[0] = _dup_kv_heads(kc, C_HEADS // C_KV_HEADS).astype(BF16)
    vc_ref[0] = _dup_kv_heads(vc, C_HEADS // C_KV_HEADS).astype(BF16)


def _in_projection(x, sh, sc, w_bf16, qn, kn, gq, gk, cos, sin, *, rope):
    b, t, d = x.shape
    tm = min(t, 512)
    row = lambda i, j: (i, j, 0)
    per_b = lambda i, j: (i, 0, 0)
    const = lambda i, j: (0, 0)
    tab = (lambda i, j: (j, 0)) if rope else const
    outs = [(A_Q, BF16), (C_Q, BF16), (B_W, F32), (4 * A_KV, BF16), (4 * A_KV, BF16),
            (2 * C_KV, BF16), (2 * C_KV, BF16)]
    return pl.pallas_call(
        functools.partial(_inproj_body, rope=rope),
        grid=(b, t // tm),
        in_specs=[
            pl.BlockSpec((1, tm, d), row),
            pl.BlockSpec((1, 1, d), per_b),
            pl.BlockSpec((1, 1, d), per_b),
            pl.BlockSpec((d, IN_WIDTH), const),
            pl.BlockSpec((1, A_Q), const),
            pl.BlockSpec((1, A_KV), const),
            pl.BlockSpec((A_Q, A_Q), const),
            pl.BlockSpec((A_KV, A_KV), const),
            pl.BlockSpec((tm, LANES), tab),
            pl.BlockSpec((tm, LANES), tab),
        ],
        out_specs=[pl.BlockSpec((1, tm, w), row) for w, _ in outs],
        out_shape=[jax.ShapeDtypeStruct((b, t, w), dt) for w, dt in outs],
        compiler_params=_params(("arbitrary", "arbitrary")),
        name="in_projection",
    )(x, sh, sc, w_bf16, qn, kn, gq, gk, cos, sin)


def _attn_body(*refs, heads_per_kv, mode, has_sink, tq, t_lat, kc_full):
    refs = list(refs)
    q_ref, kctx_ref, vctx_ref = refs[:3]
    pos = 3
    klat_ref = vlat_ref = sink_ref = None
    if mode != "none":
        klat_ref, vlat_ref = refs[pos:pos + 2]
        pos += 2
    if has_sink:
        sink_ref = refs[pos]
        pos += 1
    o_ref, kmax_ref = refs[pos:pos + 2]

    width = heads_per_kv * HEAD_DIM
    kvh = pl.program_id(1)
    qi = pl.program_id(2)
    q = q_ref[0]
    lane = lax.broadcasted_iota(jnp.int32, (1, width), 1)
    head_lanes = [(lane // HEAD_DIM) == g for g in range(heads_per_kv)]
    qs = [jnp.where(hl, q, jnp.zeros_like(q)) for hl in head_lanes]
    sinks = [sink_ref[kvh * heads_per_kv + g] * LOG2E for g in range(heads_per_kv)] if has_sink else None

    def per_head_lanes(cols):
        full = jnp.broadcast_to(cols[0], (tq, width))
        for g in range(1, heads_per_kv):
            full = jnp.where(head_lanes[g], cols[g], full)
        return full

    def key_sq_norm_max(ref):
        rows = ref.shape[1]
        step_rows = min(rows, 1024)

        def body(c, best):
            kf = ref[0, pl.ds(pl.multiple_of(c * step_rows, step_rows), step_rows), :].astype(F32)
            return jnp.maximum(best, jnp.max(jnp.sum(kf * kf, axis=1, keepdims=True), axis=0, keepdims=True))
        best = lax.fori_loop(0, rows // step_rows, body, jnp.zeros((1, 1), F32))
        return best[0, 0] * (1.0 / heads_per_kv)

    @pl.when(qi == 0)
    def _():
        best = key_sq_norm_max(kctx_ref)
        if mode != "none":
            best = jnp.maximum(best, key_sq_norm_max(klat_ref))
        kmax_ref[0] = best

    def sweep(stabilised):
        def step(carry, k, v, mask=None):
            ms, ls, acc = carry
            new_ms, new_ls, alphas = [], [], []
            contrib = None
            for g in range(heads_per_kv):
                s = lax.dot_general(qs[g], k, NT_DIMS, preferred_element_type=F32)
                if mask is not None:
                    s = jnp.where(mask, s, NEG_INF)
                if stabilised:
                    m_new = jnp.maximum(ms[g], jnp.max(s, axis=1, keepdims=True))
                    alpha = jnp.exp2(ms[g] - m_new)
                    p = jnp.exp2(s - m_new)
                    new_ls.append(alpha * ls[g] + jnp.sum(p, axis=1, keepdims=True))
                    alphas.append(alpha)
                else:
                    m_new = ms[g]
                    p = jnp.exp2(s)
                    new_ls.append(ls[g] + jnp.sum(p, axis=1, keepdims=True))
                new_ms.append(m_new)
                vg = jnp.where(head_lanes[g], v, jnp.zeros_like(v))
                d = jnp.dot(p.astype(BF16), vg, preferred_element_type=F32)
                contrib = d if contrib is None else contrib + d
            acc = acc * per_head_lanes(alphas) + contrib if stabilised else acc + contrib
            return tuple(new_ms), tuple(new_ls), acc

        if not has_sink:
            ms0 = tuple(jnp.full((tq, 1), NEG_INF if stabilised else 0.0, F32) for _ in range(heads_per_kv))
            ls0 = tuple(jnp.zeros((tq, 1), F32) for _ in range(heads_per_kv))
        elif stabilised:
            ms0 = tuple(jnp.full((tq, 1), sinks[g], F32) for g in range(heads_per_kv))
            ls0 = tuple(jnp.ones((tq, 1), F32) for _ in range(heads_per_kv))
        else:
            ms0 = tuple(jnp.zeros((tq, 1), F32) for _ in range(heads_per_kv))
            ls0 = tuple(jnp.exp2(jnp.full((tq, 1), sinks[g], F32)) for g in range(heads_per_kv))
        carry = (ms0, ls0, jnp.zeros((tq, width), F32))
        carry = step(carry, kctx_ref[0], vctx_ref[0])
        if mode == "full":
            for c in range(t_lat // kc_full):
                rows = pl.ds(c * kc_full, kc_full)
                carry = step(carry, klat_ref[0, rows, :], vlat_ref[0, rows, :])
        elif mode == "window":
            span = tq + 2 * WINDOW
            q0 = qi * tq
            k0 = pl.multiple_of(jnp.clip(q0 - WINDOW, 0, t_lat - span), WINDOW)
            qpos = q0 + lax.broadcasted_iota(jnp.int32, (tq, 1), 0)
            kpos = k0 + lax.broadcasted_iota(jnp.int32, (1, span), 1)
            in_window = jnp.abs(kpos - qpos) <= WINDOW
            carry = step(carry, klat_ref[0, pl.ds(k0, span), :], vlat_ref[0, pl.ds(k0, span), :], in_window)
        _, ls, acc = carry
        return acc * per_head_lanes([1.0 / l for l in ls])

    qf = q.astype(F32)
    qf = qf * qf
    head_sq = [jnp.sum(jnp.where(hl, qf, 0.0), axis=1, keepdims=True) for hl in head_lanes]
    q_sq = jnp.max(functools.reduce(jnp.maximum, head_sq), axis=0, keepdims=True)[0, 0]
    bounded = q_sq * kmax_ref[0] <= SAFE_LOGIT * SAFE_LOGIT
    if has_sink:
        for g in range(heads_per_kv):
            bounded = jnp.logical_and(bounded, jnp.abs(sinks[g]) <= SAFE_LOGIT)
    out = lax.cond(bounded, lambda: sweep(False), lambda: sweep(True))
    o_ref[0] = out.astype(o_ref.dtype)


def _attention(q, kctx, vctx, klat=None, vlat=None, sink=None, *, heads_per_kv, mode):
    b, t, _ = q.shape
    n_kv = 2
    width = heads_per_kv * HEAD_DIM
    t_ctx = kctx.shape[1]
    tq = min(t, 256)
    qmap = lambda i, h, j: (i, j, h)
    kmap = lambda i, h, j: (i, 0, h)
    in_specs = [pl.BlockSpec((1, tq, width), qmap),
                pl.BlockSpec((1, t_ctx, width), kmap),
                pl.BlockSpec((1, t_ctx, width), kmap)]
    args = [q, kctx, vctx]
    t_lat = 0
    if mode != "none":
        t_lat = klat.shape[1]
        in_specs += [pl.BlockSpec((1, t_lat, width), kmap), pl.BlockSpec((1, t_lat, width), kmap)]
        args += [klat, vlat]
    if sink is not None:
        in_specs.append(pl.BlockSpec(memory_space=pltpu.SMEM))
        args.append(sink)
    body = functools.partial(_attn_body, heads_per_kv=heads_per_kv, mode=mode, has_sink=sink is not None,
                             tq=tq, t_lat=t_lat, kc_full=min(max(t_lat, 1), 1024))
    return pl.pallas_call(
        body,
        grid=(b, n_kv, t // tq),
        in_specs=in_specs,
        out_specs=pl.BlockSpec((1, tq, width), qmap),
        out_shape=jax.ShapeDtypeStruct((b, t, n_kv * width), BF16),
        scratch_shapes=[pltpu.SMEM((1,), F32)],
        compiler_params=_params(("arbitrary", "arbitrary", "arbitrary")),
        name=f"attention_{mode}_g{heads_per_kv}",
    )(*args)


def _four_chan_body(u_ref, wblk_ref, cc_ref, sc_ref, z_ref):
    w = wblk_ref[...]
    wc = jnp.dot(cc_ref[...], w, precision=HIGHEST, preferred_element_type=F32)
    ws = jnp.dot(sc_ref[...], w, precision=HIGHEST, preferred_element_type=F32)
    u = u_ref[0]
    z_ref[0, 0] = jnp.dot(u, wc, precision=HIGHEST, preferred_element_type=F32)
    z_ref[0, 1] = -jnp.dot(u, ws, precision=HIGHEST, preferred_element_type=F32)


def _four_rows_body(z_ref, m_ref, tc_ref, ts_ref, a_ref):
    n1 = z_ref.shape[2]
    z = jnp.concatenate([z_ref[0, 0], z_ref[0, 1]], axis=0)
    a = jnp.dot(m_ref[...], z, precision=HIGHEST, preferred_element_type=F32)
    ar, ai = a[:n1], a[n1:]
    tc, ts = tc_ref[...], ts_ref[...]
    a_ref[0, 0] = ar * tc + ai * ts
    a_ref[0, 1] = ai * tc - ar * ts


def _four_cols_body(a_ref, cs_ref, bias_ref, y_ref):
    kb = a_ref.shape[2]
    for j in range(kb):
        rhs = jnp.concatenate([a_ref[0, 0, j], a_ref[0, 1, j]], axis=0)
        res = jnp.dot(cs_ref[...], rhs, precision=HIGHEST, preferred_element_type=F32)
        y_ref[0, :, j, :] = res + bias_ref[...]


def _dft_constants(t):
    n = int(round(t ** 0.5))
    assert n * n == t
    k = np.arange(n)
    ang = 2.0 * np.pi * np.outer(k, k) / n
    c, s = np.cos(ang), np.sin(ang)
    rows = np.block([[c, s], [-s, c]]).astype(np.float32)
    cols = np.concatenate([c, s], axis=1).astype(np.float32)
    tw = 2.0 * np.pi * np.outer(k, k) / t
    tc = np.repeat(np.cos(tw), B_W, axis=1).astype(np.float32)
    ts = np.repeat(np.sin(tw), B_W, axis=1).astype(np.float32)
    return n, rows, cols, tc, ts


def _channel_constants(t):
    k = np.arange(B_GROUP_DIM)
    ang = 2.0 * np.pi * np.outer(k, k) / B_GROUP_DIM
    norm = 1.0 / np.sqrt(float(t) * B_GROUP_DIM)
    eye = np.eye(B_GROUPS)
    cc = np.kron(eye, np.cos(ang) * norm).astype(np.float32)
    sc = np.kron(eye, np.sin(ang) * norm).astype(np.float32)
    return cc, sc


def _fourier_mix(u, w_blockdiag, bias):
    b, t, _ = u.shape
    n, rows, cols, tc, ts = _dft_constants(t)
    cc, sc = _channel_constants(t)
    tm = min(t, 1024)
    z = pl.pallas_call(
        _four_chan_body,
        grid=(b, t // tm),
        in_specs=[pl.BlockSpec((1, tm, B_W), lambda i, j: (i, j, 0)),
                  pl.BlockSpec((B_W, B_W), lambda i, j: (0, 0)),
                  pl.BlockSpec((B_W, B_W), lambda i, j: (0, 0)),
                  pl.BlockSpec((B_W, B_W), lambda i, j: (0, 0))],
        out_specs=pl.BlockSpec((1, 2, tm, B_W), lambda i, j: (i, 0, j, 0)),
        out_shape=jax.ShapeDtypeStruct((b, 2, t, B_W), F32),
        compiler_params=_params(("arbitrary", "arbitrary")),
        name="fourier_channels",
    )(u, w_blockdiag, jnp.asarray(cc), jnp.asarray(sc))
    cols_total = n * B_W
    tc_cols = min(cols_total, 4096)
    a = pl.pallas_call(
        _four_rows_body,
        grid=(b, cols_total // tc_cols),
        in_specs=[pl.BlockSpec((1, 2, n, tc_cols), lambda i, j: (i, 0, 0, j)),
                  pl.BlockSpec((2 * n, 2 * n), lambda i, j: (0, 0)),
                  pl.BlockSpec((n, tc_cols), lambda i, j: (0, j)),
                  pl.BlockSpec((n, tc_cols), lambda i, j: (0, j))],
        out_specs=pl.BlockSpec((1, 2, n, tc_cols), lambda i, j: (i, 0, 0, j)),
        out_shape=jax.ShapeDtypeStruct((b, 2, n, cols_total), F32),
        compiler_params=_params(("arbitrary", "arbitrary")),
        name="fourier_rows",
    )(z.reshape(b, 2, n, cols_total), jnp.asarray(rows), jnp.asarray(tc), jnp.asarray(ts))
    kb = 8
    y = pl.pallas_call(
        _four_cols_body,
        grid=(b, n // kb),
        in_specs=[pl.BlockSpec((1, 2, kb, n, B_W), lambda i, j: (i, 0, j, 0, 0)),
                  pl.BlockSpec((n, 2 * n), lambda i, j: (0, 0)),
                  pl.BlockSpec((1, B_W), lambda i, j: (0, 0))],
        out_specs=pl.BlockSpec((1, n, kb, B_W), lambda i, j: (i, 0, j, 0)),
        out_shape=jax.ShapeDtypeStruct((b, n, n, B_W), F32),
        compiler_params=_params(("arbitrary", "arbitrary")),
        name="fourier_cols",
    )(a.reshape(b, 2, n, n, B_W), jnp.asarray(cols), bias)
    return y.reshape(b, t, B_W)


def _outproj_body(a_ref, f_ref, c_ref, w_ref, x_ref, g1_ref, sh2_ref, sc2_ref, lng_ref, lnb_ref, wr_ref,
                  x1_ref, h2_ref, aff_ref):
    o = jnp.dot(a_ref[0], w_ref[:A_Q], preferred_element_type=F32)
    o = o + jnp.dot(f_ref[0].astype(BF16), w_ref[A_Q:A_Q + B_W], preferred_element_type=F32)
    o = o + jnp.dot(c_ref[0], w_ref[A_Q + B_W:], preferred_element_type=F32)
    x1 = _layer_norm(DEEPNORM_ALPHA * x_ref[0] + g1_ref[0] * o) * lng_ref[...] + lnb_ref[...]
    x1_ref[0] = x1
    h2 = _layer_norm(x1) * (1.0 + sc2_ref[0]) + sh2_ref[0]
    h2_ref[0] = h2
    logits = lax.dot_general(wr_ref[...], h2, NT_DIMS, precision=HIGHEST, preferred_element_type=F32)
    e = jnp.exp(logits - jnp.max(logits, axis=0, keepdims=True))
    aff_ref[0] = e / jnp.sum(e, axis=0, keepdims=True)


def _out_projection(oa, of, oc, w_bf16, x, g1, sh2, sc2, lng, lnb, wr_t):
    b, t, d = x.shape
    tm = min(t, 512)
    row = lambda i, j: (i, j, 0)
    per_b = lambda i, j: (i, 0, 0)
    const = lambda i, j: (0, 0)
    return pl.pallas_call(
        _outproj_body,
        grid=(b, t // tm),
        in_specs=[pl.BlockSpec((1, tm, A_Q), row),
                  pl.BlockSpec((1, tm, B_W), row),
                  pl.BlockSpec((1, tm, C_Q), row),
                  pl.BlockSpec((d, d), const),
                  pl.BlockSpec((1, tm, d), row),
                  pl.BlockSpec((1, 1, d), per_b),
                  pl.BlockSpec((1, 1, d), per_b),
                  pl.BlockSpec((1, 1, d), per_b),
                  pl.BlockSpec((1, d), const),
                  pl.BlockSpec((1, d), const),
                  pl.BlockSpec((N_EXPERTS, d), const)],
        out_specs=[pl.BlockSpec((1, tm, d), row),
                   pl.BlockSpec((1, tm, d), row),
                   pl.BlockSpec((1, N_EXPERTS, tm), lambda i, j: (i, 0, j))],
        out_shape=[jax.ShapeDtypeStruct((b, t, d), F32),
                   jax.ShapeDtypeStruct((b, t, d), F32),
                   jax.ShapeDtypeStruct((b, N_EXPERTS, t), F32)],
        compiler_params=_params(("arbitrary", "arbitrary")),
        name="out_projection",
    )(oa, of, oc, w_bf16, x, g1, sh2, sc2, lng, lnb, wr_t)


def _ffn_body(x_ref, wg_ref, wu_ref, wd_ref, gate_ref, o_ref):
    f = pl.program_id(2)
    x = x_ref[0]
    a = jnp.dot(x, wg_ref[0].astype(BF16), preferred_element_type=F32)
    up = jnp.dot(x, wu_ref[0].astype(BF16), preferred_element_type=F32)
    h = (_silu(a) * up).astype(BF16)
    y = jnp.dot(h, wd_ref[0].astype(BF16), preferred_element_type=F32)

    @pl.when(f == 0)
    def _():
        o_ref[0] = y

    @pl.when(f > 0)
    def _():
        o_ref[0] += y

    @pl.when(f == pl.num_programs(2) - 1)
    def _():
        o_ref[0] *= gate_ref[0]


def _expert_ffn(xg, w_gate, w_up, w_down, gate):
    e, m, d = xg.shape
    ff = w_gate.shape[-1]
    tm = min(m, 2048)
    tf = 512
    return pl.pallas_call(
        _ffn_body,
        grid=(e, m // tm, ff // tf),
        in_specs=[pl.BlockSpec((1, tm, d), lambda i, j, k: (i, j, 0)),
                  pl.BlockSpec((1, d, tf), lambda i, j, k: (i, 0, k)),
                  pl.BlockSpec((1, d, tf), lambda i, j, k: (i, 0, k)),
                  pl.BlockSpec((1, tf, d), lambda i, j, k: (i, k, 0)),
                  pl.BlockSpec((1, tm, 1), lambda i, j, k: (i, j, 0))],
        out_specs=pl.BlockSpec((1, tm, d), lambda i, j, k: (i, j, 0)),
        out_shape=jax.ShapeDtypeStruct((e, m, d), F32),
        compiler_params=_params(("arbitrary", "arbitrary", "arbitrary")),
        name="expert_ffn",
    )(xg, w_gate, w_up, w_down, gate)


def _postnorm_body(x_ref, y_ref, g_ref, lng_ref, lnb_ref, o_ref):
    o_ref[0] = _layer_norm(DEEPNORM_ALPHA * x_ref[0] + g_ref[0] * y_ref[0]) * lng_ref[...] + lnb_ref[...]


def _post_norm(x, y, g, lng, lnb):
    b, t, d = x.shape
    tm = min(t, 1024)
    row = lambda i, j: (i, j, 0)
    return pl.pallas_call(
        _postnorm_body,
        grid=(b, t // tm),
        in_specs=[pl.BlockSpec((1, tm, d), row), pl.BlockSpec((1, tm, d), row),
                  pl.BlockSpec((1, 1, d), lambda i, j: (i, 0, 0)),
                  pl.BlockSpec((1, d), lambda i, j: (0, 0)), pl.BlockSpec((1, d), lambda i, j: (0, 0))],
        out_specs=pl.BlockSpec((1, tm, d), row),
        out_shape=jax.ShapeDtypeStruct((b, t, d), F32),
        compiler_params=_params(("arbitrary", "arbitrary")),
        name="post_norm",
    )(x, y, g, lng, lnb)


def _rope_tables(t):
    tok = jnp.arange(t, dtype=jnp.int32)
    lane = np.arange(LANES)
    d = lane % HEAD_DIM
    inv_freq = ROPE_BASE ** (-jnp.arange(0, HEAD_DIM // 2, 2, dtype=F32) / (HEAD_DIM // 2))
    freq = inv_freq[d % 16]
    use_col = jnp.asarray((d // 32) == 1)
    position = jnp.where(use_col[None, :], (tok % GRID_W)[:, None], (tok // GRID_W)[:, None])
    ang = position * freq[None, :]
    sign = jnp.asarray(np.where((d % 32) < 16, -1.0, 1.0).astype(np.float32))
    return jnp.cos(ang).astype(F32), (jnp.sin(ang) * sign[None, :]).astype(F32)


def _block_ones(width):
    idx = np.arange(width) // HEAD_DIM
    return jnp.asarray((idx[:, None] == idx[None, :]).astype(np.float32), dtype=BF16)


def _moe(h2, aff, w_gate, w_up, w_down):
    b, t, d = h2.shape
    cap = EC_CAPACITY_FACTOR * t // N_EXPERTS
    gate, idx = lax.top_k(aff, cap)
    xg = jax.vmap(lambda hb, ib: hb[ib])(h2, idx).astype(BF16)
    xg = jnp.swapaxes(xg, 0, 1).reshape(N_EXPERTS, b * cap, d)
    gate_e = jnp.swapaxes(gate, 0, 1).reshape(N_EXPERTS, b * cap, 1)
    y = _expert_ffn(xg, w_gate, w_up, w_down, gate_e)
    y = jnp.swapaxes(y.reshape(N_EXPERTS, b, cap, d), 0, 1)
    return jax.vmap(lambda ib, yb: jnp.zeros((t, d), F32).at[ib.reshape(-1)].add(yb.reshape(-1, d)))(idx, y)


def kernel(x, c, ctx, c_ctx, w_mod, b_mod, w_in, q_norm_a, k_norm_a, w_fourier, b_fourier, sink_c, w_out, ln1_g,
           ln1_b, w_router, w_gate, w_up, w_down, ln2_g, ln2_b):
    b, t, d = x.shape
    cos, sin = _rope_tables(t)
    gq, gk = _block_ones(A_Q), _block_ones(A_KV)
    cc = jnp.zeros((MOD_ROWS, d), F32).at[:b].set(c).at[b].set(c_ctx)
    mod = _modulation(cc, w_mod, b_mod)
    w_in_bf = w_in.astype(BF16)
    w_out_bf = w_out.astype(BF16)

    x_lat, x_ctx = x, ctx
    for layer in range(DEPTH):
        update_ctx = layer < DEPTH - 1
        lat = [mod[layer, :b, i * d:(i + 1) * d][:, None, :] for i in range(6)]
        cm = [jnp.broadcast_to(mod[layer, b, i * d:(i + 1) * d][None, None, :], (b, 1, d)) for i in range(6)]
        qn = jnp.tile(q_norm_a[layer], A_HEADS)[None, :]
        kn = jnp.tile(k_norm_a[layer], A_KV_HEADS)[None, :]
        w_four = jax.scipy.linalg.block_diag(*[w_fourier[layer, g] for g in range(B_GROUPS)])
        bias = b_fourier[layer].reshape(1, B_W)
        lng1, lnb1 = ln1_g[layer][None, :], ln1_b[layer][None, :]
        lng2, lnb2 = ln2_g[layer][None, :], ln2_b[layer][None, :]
        wr_t = w_router[layer].T
        sink = sink_c[layer]

        proj_ctx = _in_projection(x_ctx, cm[0], cm[1], w_in_bf[layer], qn, kn, gq, gk, cos, sin, rope=False)
        qa_c, qc_c, u_c, ka_c, va_c, kc_c, vc_c = proj_ctx
        qa, qc, u, ka, va, kc, vc = _in_projection(x_lat, lat[0], lat[1], w_in_bf[layer], qn, kn, gq, gk, cos, sin,
                                                   rope=True)
        out_a = _attention(qa, ka_c, va_c, ka, va, heads_per_kv=4, mode="full")
        out_b = _fourier_mix(u, w_four, bias)
        out_c = _attention(qc, kc_c, vc_c, kc, vc, sink, heads_per_kv=2, mode="window")
        x1, h2, aff = _out_projection(out_a, out_b, out_c, w_out_bf[layer], x_lat, lat[2], lat[3], lat[4],
                                      lng1, lnb1, wr_t)
        y = _moe(h2, aff, w_gate[layer], w_up[layer], w_down[layer])
        x_lat = _post_norm(x1, y, lat[5], lng2, lnb2)

        if update_ctx:
            out_a_c = _attention(qa_c, ka_c, va_c, heads_per_kv=4, mode="none")
            out_b_c = _fourier_mix(u_c, w_four, bias)
            out_c_c = _attention(qc_c, kc_c, vc_c, sink=sink, heads_per_kv=2, mode="none")
            x1c, h2c, affc = _out_projection(out_a_c, out_b_c, out_c_c, w_out_bf[layer], x_ctx, cm[2], cm[3], cm[4],
                                             lng1, lnb1, wr_t)
            yc = _moe(h2c, affc, w_gate[layer], w_up[layer], w_down[layer])
            x_ctx = _post_norm(x1c, yc, cm[5], lng2, lnb2)
    return x_lat
```

```python
import functools

import numpy as np
import jax
import jax.numpy as jnp
from jax import lax
from jax.experimental import pallas as pl
from jax.experimental.pallas import tpu as pltpu

F32 = jnp.float32
BF16 = jnp.bfloat16

D_MODEL = 1024
DEPTH = 2
GRID_W = 64
HEAD_DIM = 64
ROPE_BASE = 10000.0
A_HEADS, A_KV_HEADS = 8, 2
B_GROUPS, B_GROUP_DIM = 4, 64
C_HEADS, C_KV_HEADS = 4, 2
WINDOW = 128
N_EXPERTS = 16
EC_CAPACITY_FACTOR = 2
EXPERT_FF = 2 * D_MODEL
A_Q, A_KV, B_W, C_Q, C_KV = 512, 128, 256, 256, 128
QU_WIDTH = A_Q + C_Q + B_W
IN_WIDTH = QU_WIDTH + 2 * A_KV + 2 * C_KV
LN_EPS = 1e-5
RMS_EPS = 1e-6
NEG_INF = -1e30
LOG2E = 1.4426950408889634
SAFE_LOGIT = 60.0
DEEPNORM_ALPHA = (2 * DEPTH) ** 0.25
MOD_ROWS = 16
LANES = 128
VMEM_LIMIT = 56 * 1024 * 1024

HIGHEST = lax.Precision.HIGHEST
NT_DIMS = (((1,), (1,)), ((), ()))


def _params(sem):
    return pltpu.CompilerParams(dimension_semantics=sem, vmem_limit_bytes=VMEM_LIMIT)


def _layer_norm(x):
    mu = jnp.mean(x, axis=-1, keepdims=True)
    xc = x - mu
    var = jnp.mean(xc * xc, axis=-1, keepdims=True)
    return xc * lax.rsqrt(var + LN_EPS)


def _silu(x):
    return x * (1.0 / (1.0 + jnp.exp(-x)))


def _mod_body(c_ref, w_ref, b_ref, o_ref):
    s = _silu(c_ref[...])
    o_ref[0] = jnp.dot(s, w_ref[0], precision=HIGHEST, preferred_element_type=F32) + b_ref[0]


def _modulation(cc, w_mod, b_mod):
    tn = 1024
    n = w_mod.shape[-1]
    return pl.pallas_call(
        _mod_body,
        grid=(DEPTH, n // tn),
        in_specs=[
            pl.BlockSpec((MOD_ROWS, D_MODEL), lambda l, j: (0, 0)),
            pl.BlockSpec((1, D_MODEL, tn), lambda l, j: (l, 0, j)),
            pl.BlockSpec((1, 1, tn), lambda l, j: (l, 0, j)),
        ],
        out_specs=pl.BlockSpec((1, MOD_ROWS, tn), lambda l, j: (l, 0, j)),
        out_shape=jax.ShapeDtypeStruct((DEPTH, MOD_ROWS, n), F32),
        compiler_params=_params(("arbitrary", "arbitrary")),
        name="modulation",
    )(cc, w_mod, b_mod.reshape(DEPTH, 1, n))


def _rope(x, cos, sin_signed, first_half):
    outs = []
    for j in range(x.shape[1] // LANES):
        xj = x[:, LANES * j:LANES * (j + 1)]
        partner = jnp.where(first_half, pltpu.roll(xj, LANES - 16, 1), pltpu.roll(xj, 16, 1))
        outs.append(xj * cos + partner * sin_signed)
    return jnp.concatenate(outs, axis=1) if len(outs) > 1 else outs[0]


def _dup_kv_heads(k, copies):
    lane = lax.broadcasted_iota(jnp.int32, (1, LANES), 1)
    lo = lane < HEAD_DIM
    r = pltpu.roll(k, HEAD_DIM, 1)
    d0 = jnp.where(lo, k, r)
    d1 = jnp.where(lo, r, k)
    reps = copies // 2
    return jnp.concatenate([d0] * reps + [d1] * reps, axis=1)


def _inproj_body(x_ref, sh_ref, sc_ref, w_ref, qn_ref, kn_ref, gq_ref, gk_ref, cos_ref, sin_ref,
                 qa_ref, qc_ref, u_ref, ka_ref, va_ref, kc_ref, vc_ref, *, rope):
    h = _layer_norm(x_ref[0]) * (1.0 + sc_ref[0]) + sh_ref[0]
    p = jnp.dot(h.astype(BF16), w_ref[...], preferred_element_type=F32)
    qa = p[:, :A_Q]
    qc = p[:, A_Q:A_Q + C_Q]
    u = p[:, A_Q + C_Q:QU_WIDTH]
    ka = p[:, QU_WIDTH:QU_WIDTH + A_KV]
    va = p[:, QU_WIDTH + A_KV:QU_WIDTH + 2 * A_KV]
    kc = p[:, QU_WIDTH + 2 * A_KV:QU_WIDTH + 2 * A_KV + C_KV]
    vc = p[:, QU_WIDTH + 2 * A_KV + C_KV:]
    msq = jnp.dot((qa * qa).astype(BF16), gq_ref[...], preferred_element_type=F32) * (1.0 / HEAD_DIM)
    qa = qa * lax.rsqrt(msq + RMS_EPS) * qn_ref[...]
    msk = jnp.dot((ka * ka).astype(BF16), gk_ref[...], preferred_element_type=F32) * (1.0 / HEAD_DIM)
    ka = ka * lax.rsqrt(msk + RMS_EPS) * kn_ref[...]
    if rope:
        cos = cos_ref[...]
        sin = sin_ref[...]
        lane = lax.broadcasted_iota(jnp.int32, (1, LANES), 1)
        first_half = (lane % 32) < 16
        qa = _rope(qa, cos, sin, first_half)
        qc = _rope(qc, cos, sin, first_half)
        ka = _rope(ka, cos, sin, first_half)
        kc = _rope(kc, cos, sin, first_half)
    scale = HEAD_DIM ** -0.5 * LOG2E
    qa_ref[0] = (qa * scale).astype(BF16)
    qc_ref[0] = (qc * scale).astype(BF16)
    u_ref[0] = u
    ka_ref[0] = _dup_kv_heads(ka, A_HEADS // A_KV_HEADS).astype(BF16)
    va_ref[0] = _dup_kv_heads(va, A_HEADS // A_KV_HEADS).astype(BF16)
    kc_ref[0] = _dup_kv_heads(kc, C_HEADS // C_KV_HEADS).astype(BF16)
    vc_ref[0] = _dup_kv_heads(vc, C_HEADS // C_KV_HEADS).astype(BF16)


def _in_projection(x, sh, sc, w_bf16, qn, kn, gq, gk, cos, sin, *, rope):
    b, t, d = x.shape
    tm = min(t, 512)
    row = lambda i, j: (i, j, 0)
    per_b = lambda i, j: (i, 0, 0)
    const = lambda i, j: (0, 0)
    tab = (lambda i, j: (j, 0)) if rope else const
    outs = [(A_Q, BF16), (C_Q, BF16), (B_W, F32), (4 * A_KV, BF16), (4 * A_KV, BF16),
            (2 * C_KV, BF16), (2 * C_KV, BF16)]
    return pl.pallas_call(
        functools.partial(_inproj_body, rope=rope),
        grid=(b, t // tm),
        in_specs=[
            pl.BlockSpec((1, tm, d), row),
            pl.BlockSpec((1, 1, d), per_b),
            pl.BlockSpec((1, 1, d), per_b),
            pl.BlockSpec((d, IN_WIDTH), const),
            pl.BlockSpec((1, A_Q), const),
            pl.BlockSpec((1, A_KV), const),
            pl.BlockSpec((A_Q, A_Q), const),
            pl.BlockSpec((A_KV, A_KV), const),
            pl.BlockSpec((tm, LANES), tab),
            pl.BlockSpec((tm, LANES), tab),
        ],
        out_specs=[pl.BlockSpec((1, tm, w), row) for w, _ in outs],
        out_shape=[jax.ShapeDtypeStruct((b, t, w), dt) for w, dt in outs],
        compiler_params=_params(("arbitrary", "arbitrary")),
        name="in_projection",
    )(x, sh, sc, w_bf16, qn, kn, gq, gk, cos, sin)


def _attn_body(*refs, heads_per_kv, mode, has_sink, tq, t_lat, kc_full):
    refs = list(refs)
    q_ref, kctx_ref, vctx_ref = refs[:3]
    pos = 3
    klat_ref = vlat_ref = sink_ref = None
    if mode != "none":
        klat_ref, vlat_ref = refs[pos:pos + 2]
        pos += 2
    if has_sink:
        sink_ref = refs[pos]
        pos += 1
    o_ref, kmax_ref = refs[pos:pos + 2]

    width = heads_per_kv * HEAD_DIM
    kvh = pl.program_id(1)
    qi = pl.program_id(2)
    q = q_ref[0]
    lane = lax.broadcasted_iota(jnp.int32, (1, width), 1)
    head_lanes = [(lane // HEAD_DIM) == g for g in range(heads_per_kv)]
    q_stack = jnp.concatenate([jnp.where(hl, q, jnp.zeros_like(q)) for hl in head_lanes], axis=0)
    sinks =[sink_ref[kvh * heads_per_kv + g] * LOG2E for g in range(heads_per_kv)] if has_sink else None

    def per_head_lanes(cols):
        full = jnp.broadcast_to(cols[0], (tq, width))
        for g in range(1, heads_per_kv):
            full = jnp.where(head_lanes[g], cols[g], full)
        return full

    def key_sq_norm_max(ref):
        rows = ref.shape[1]
        step_rows = min(rows, 1024)

        def body(c, best):
            kf = ref[0, pl.ds(pl.multiple_of(c * step_rows, step_rows), step_rows), :].astype(F32)
            return jnp.maximum(best, jnp.max(jnp.sum(kf * kf, axis=1, keepdims=True), axis=0, keepdims=True))
        best = lax.fori_loop(0, rows // step_rows, body, jnp.zeros((1, 1), F32))
        return best[0, 0] * (1.0 / heads_per_kv)

    @pl.when(qi == 0)
    def _():
        best = key_sq_norm_max(kctx_ref)
        if mode != "none":
            best = jnp.maximum(best, key_sq_norm_max(klat_ref))
        kmax_ref[0] = best

    def sweep(stabilised):
        def by_head(stacked):
            return [stacked[g * tq:(g + 1) * tq] for g in range(heads_per_kv)]

        def step(carry, k, v, mask=None):
            m, l, acc = carry
            s = lax.dot_general(q_stack, k, NT_DIMS, preferred_element_type=F32)
            if mask is not None:
                s = jnp.where(jnp.concatenate([mask] * heads_per_kv, axis=0), s, NEG_INF)
            if stabilised:
                m_new = jnp.maximum(m, jnp.max(s, axis=1, keepdims=True))
                alpha = jnp.exp2(m - m_new)
                p = jnp.exp2(s - m_new)
                l = alpha * l + jnp.sum(p, axis=1, keepdims=True)
                acc = acc * per_head_lanes(by_head(alpha))
            else:
                m_new = m
                p = jnp.exp2(s)
                l = l + jnp.sum(p, axis=1, keepdims=True)
            pv = jnp.dot(p.astype(BF16), v, preferred_element_type=F32)
            for g, part in enumerate(by_head(pv)):
                acc = acc + jnp.where(head_lanes[g], part, 0.0)
            return m_new, l, acc

        rows_of = lambda vals: jnp.concatenate([jnp.full((tq, 1), v, F32) for v in vals], axis=0)
        if not has_sink:
            m0 = jnp.full((heads_per_kv * tq, 1), NEG_INF if stabilised else 0.0, F32)
            l0 = jnp.zeros((heads_per_kv * tq, 1), F32)
        elif stabilised:
            m0 = rows_of(sinks)
            l0 = jnp.ones((heads_per_kv * tq, 1), F32)
        else:
            m0 = jnp.zeros((heads_per_kv * tq, 1), F32)
            l0 = jnp.exp2(rows_of(sinks))
        carry = (m0, l0, jnp.zeros((tq, width), F32))
        carry = step(carry, kctx_ref[0], vctx_ref[0])
        if mode == "full":
            for c in range(t_lat // kc_full):
                rows = pl.ds(c * kc_full, kc_full)
                carry = step(carry, klat_ref[0, rows, :], vlat_ref[0, rows, :])
        elif mode == "window":
            span = tq + 2 * WINDOW
            q0 = qi * tq
            k0 = pl.multiple_of(jnp.clip(q0 - WINDOW, 0, t_lat - span), WINDOW)
            qpos = q0 + lax.broadcasted_iota(jnp.int32, (tq, 1), 0)
            kpos = k0 + lax.broadcasted_iota(jnp.int32, (1, span), 1)
            in_window = jnp.abs(kpos - qpos) <= WINDOW
            carry = step(carry, klat_ref[0, pl.ds(k0, span), :], vlat_ref[0, pl.ds(k0, span), :], in_window)
        _, l, acc = carry
        return acc * per_head_lanes(by_head(1.0 / l))

    qf = q.astype(F32)
    qf = qf * qf
    head_sq = [jnp.sum(jnp.where(hl, qf, 0.0), axis=1, keepdims=True) for hl in head_lanes]
    q_sq = jnp.max(functools.reduce(jnp.maximum, head_sq), axis=0, keepdims=True)[0, 0]
    bounded = q_sq * kmax_ref[0] <= SAFE_LOGIT * SAFE_LOGIT
    if has_sink:
        for g in range(heads_per_kv):
            bounded = jnp.logical_and(bounded, jnp.abs(sinks[g]) <= SAFE_LOGIT)
    out = lax.cond(bounded, lambda: sweep(False), lambda: sweep(True))
    o_ref[0] = out.astype(o_ref.dtype)


def _attention(q, kctx, vctx, klat=None, vlat=None, sink=None, *, heads_per_kv, mode):
    b, t, _ = q.shape
    n_kv = 2
    width = heads_per_kv * HEAD_DIM
    t_ctx = kctx.shape[1]
    tq = min(t, 256)
    qmap = lambda i, h, j: (i, j, h)
    kmap = lambda i, h, j: (i, 0, h)
    in_specs = [pl.BlockSpec((1, tq, width), qmap),
                pl.BlockSpec((1, t_ctx, width), kmap),
                pl.BlockSpec((1, t_ctx, width), kmap)]
    args = [q, kctx, vctx]
    t_lat = 0
    if mode != "none":
        t_lat = klat.shape[1]
        in_specs += [pl.BlockSpec((1, t_lat, width), kmap), pl.BlockSpec((1, t_lat, width), kmap)]
        args += [klat, vlat]
    if sink is not None:
        in_specs.append(pl.BlockSpec(memory_space=pltpu.SMEM))
        args.append(sink)
    body = functools.partial(_attn_body, heads_per_kv=heads_per_kv, mode=mode, has_sink=sink is not None,
                             tq=tq, t_lat=t_lat, kc_full=min(max(t_lat, 1), 1024))
    return pl.pallas_call(
        body,
        grid=(b, n_kv, t // tq),
        in_specs=in_specs,
        out_specs=pl.BlockSpec((1, tq, width), qmap),
        out_shape=jax.ShapeDtypeStruct((b, t, n_kv * width), BF16),
        scratch_shapes=[pltpu.SMEM((1,), F32)],
        compiler_params=_params(("arbitrary", "arbitrary", "arbitrary")),
        name=f"attention_{mode}_g{heads_per_kv}",
    )(*args)


def _four_chan_body(u_ref, wblk_ref, cc_ref, sc_ref, z_ref):
    w = wblk_ref[...]
    wc = jnp.dot(cc_ref[...], w, precision=HIGHEST, preferred_element_type=F32)
    ws = jnp.dot(sc_ref[...], w, precision=HIGHEST, preferred_element_type=F32)
    u = u_ref[0]
    z_ref[0, 0] = jnp.dot(u, wc, precision=HIGHEST, preferred_element_type=F32)
    z_ref[0, 1] = -jnp.dot(u, ws, precision=HIGHEST, preferred_element_type=F32)


def _four_rows_body(z_ref, m_ref, tc_ref, ts_ref, a_ref):
    n1 = z_ref.shape[2]
    z = jnp.concatenate([z_ref[0, 0], z_ref[0, 1]], axis=0)
    a = jnp.dot(m_ref[...], z, precision=HIGHEST, preferred_element_type=F32)
    ar, ai = a[:n1], a[n1:]
    tc, ts = tc_ref[...], ts_ref[...]
    a_ref[0, 0] = ar * tc + ai * ts
    a_ref[0, 1] = ai * tc - ar * ts


def _four_cols_body(a_ref, cs_ref, bias_ref, y_ref):
    kb = a_ref.shape[2]
    for j in range(kb):
        rhs = jnp.concatenate([a_ref[0, 0, j], a_ref[0, 1, j]], axis=0)
        res = jnp.dot(cs_ref[...], rhs, precision=HIGHEST, preferred_element_type=F32)
        y_ref[0, :, j, :] = res + bias_ref[...]


def _dft_constants(t):
    n = int(round(t ** 0.5))
    assert n * n == t
    k = np.arange(n)
    ang = 2.0 * np.pi * np.outer(k, k) / n
    c, s = np.cos(ang), np.sin(ang)
    rows = np.block([[c, s], [-s, c]]).astype(np.float32)
    cols = np.concatenate([c, s], axis=1).astype(np.float32)
    tw = 2.0 * np.pi * np.outer(k, k) / t
    tc = np.repeat(np.cos(tw), B_W, axis=1).astype(np.float32)
    ts = np.repeat(np.sin(tw), B_W, axis=1).astype(np.float32)
    return n, rows, cols, tc, ts


def _channel_constants(t):
    k = np.arange(B_GROUP_DIM)
    ang = 2.0 * np.pi * np.outer(k, k) / B_GROUP_DIM
    norm = 1.0 / np.sqrt(float(t) * B_GROUP_DIM)
    eye = np.eye(B_GROUPS)
    cc = np.kron(eye, np.cos(ang) * norm).astype(np.float32)
    sc = np.kron(eye, np.sin(ang) * norm).astype(np.float32)
    return cc, sc


def _fourier_mix(u, w_blockdiag, bias):
    b, t, _ = u.shape
    n, rows, cols, tc, ts = _dft_constants(t)
    cc, sc = _channel_constants(t)
    tm = min(t, 1024)
    z = pl.pallas_call(
        _four_chan_body,
        grid=(b, t // tm),
        in_specs=[pl.BlockSpec((1, tm, B_W), lambda i, j: (i, j, 0)),
                  pl.BlockSpec((B_W, B_W), lambda i, j: (0, 0)),
                  pl.BlockSpec((B_W, B_W), lambda i, j: (0, 0)),
                  pl.BlockSpec((B_W, B_W), lambda i, j: (0, 0))],
        out_specs=pl.BlockSpec((1, 2, tm, B_W), lambda i, j: (i, 0, j, 0)),
        out_shape=jax.ShapeDtypeStruct((b, 2, t, B_W), F32),
        compiler_params=_params(("arbitrary", "arbitrary")),
        name="fourier_channels",
    )(u, w_blockdiag, jnp.asarray(cc), jnp.asarray(sc))
    cols_total = n * B_W
    tc_cols = min(cols_total, 4096)
    a = pl.pallas_call(
        _four_rows_body,
        grid=(b, cols_total // tc_cols),
        in_specs=[pl.BlockSpec((1, 2, n, tc_cols), lambda i, j: (i, 0, 0, j)),
                  pl.BlockSpec((2 * n, 2 * n), lambda i, j: (0, 0)),
                  pl.BlockSpec((n, tc_cols), lambda i, j: (0, j)),
                  pl.BlockSpec((n, tc_cols), lambda i, j: (0, j))],
        out_specs=pl.BlockSpec((1, 2, n, tc_cols), lambda i, j: (i, 0, 0, j)),
        out_shape=jax.ShapeDtypeStruct((b, 2, n, cols_total), F32),
        compiler_params=_params(("arbitrary", "arbitrary")),
        name="fourier_rows",
    )(z.reshape(b, 2, n, cols_total), jnp.asarray(rows), jnp.asarray(tc), jnp.asarray(ts))
    kb = 8
    y = pl.pallas_call(
        _four_cols_body,
        grid=(b, n // kb),
        in_specs=[pl.BlockSpec((1, 2, kb, n, B_W), lambda i, j: (i, 0, j, 0, 0)),
                  pl.BlockSpec((n, 2 * n), lambda i, j: (0, 0)),
                  pl.BlockSpec((1, B_W), lambda i, j: (0, 0))],
        out_specs=pl.BlockSpec((1, n, kb, B_W), lambda i, j: (i, 0, j, 0)),
        out_shape=jax.ShapeDtypeStruct((b, n, n, B_W), F32),
        compiler_params=_params(("arbitrary", "arbitrary")),
        name="fourier_cols",
    )(a.reshape(b, 2, n, n, B_W), jnp.asarray(cols), bias)
    return y.reshape(b, t, B_W)


def _outproj_body(a_ref, f_ref, c_ref, w_ref, x_ref, g1_ref, sh2_ref, sc2_ref, lng_ref, lnb_ref, wr_ref,
                  x1_ref, h2_ref, aff_ref):
    o = jnp.dot(a_ref[0], w_ref[:A_Q], preferred_element_type=F32)
    o = o + jnp.dot(f_ref[0].astype(BF16), w_ref[A_Q:A_Q + B_W], preferred_element_type=F32)
    o = o + jnp.dot(c_ref[0], w_ref[A_Q + B_W:], preferred_element_type=F32)
    x1 = _layer_norm(DEEPNORM_ALPHA * x_ref[0] + g1_ref[0] * o) * lng_ref[...] + lnb_ref[...]
    x1_ref[0] = x1
    h2 = _layer_norm(x1) * (1.0 + sc2_ref[0]) + sh2_ref[0]
    h2_ref[0] = h2
    logits = lax.dot_general(wr_ref[...], h2, NT_DIMS, precision=HIGHEST, preferred_element_type=F32)
    e = jnp.exp(logits - jnp.max(logits, axis=0, keepdims=True))
    aff_ref[0] = e / jnp.sum(e, axis=0, keepdims=True)


def _out_projection(oa, of, oc, w_bf16, x, g1, sh2, sc2, lng, lnb, wr_t):
    b, t, d = x.shape
    tm = min(t, 512)
    row = lambda i, j: (i, j, 0)
    per_b = lambda i, j: (i, 0, 0)
    const = lambda i, j: (0, 0)
    return pl.pallas_call(
        _outproj_body,
        grid=(b, t // tm),
        in_specs=[pl.BlockSpec((1, tm, A_Q), row),
                  pl.BlockSpec((1, tm, B_W), row),
                  pl.BlockSpec((1, tm, C_Q), row),
                  pl.BlockSpec((d, d), const),
                  pl.BlockSpec((1, tm, d), row),
                  pl.BlockSpec((1, 1, d), per_b),
                  pl.BlockSpec((1, 1, d), per_b),
                  pl.BlockSpec((1, 1, d), per_b),
                  pl.BlockSpec((1, d), const),
                  pl.BlockSpec((1, d), const),
                  pl.BlockSpec((N_EXPERTS, d), const)],
        out_specs=[pl.BlockSpec((1, tm, d), row),
                   pl.BlockSpec((1, tm, d), row),
                   pl.BlockSpec((1, N_EXPERTS, tm), lambda i, j: (i, 0, j))],
        out_shape=[jax.ShapeDtypeStruct((b, t, d), F32),
                   jax.ShapeDtypeStruct((b, t, d), F32),
                   jax.ShapeDtypeStruct((b, N_EXPERTS, t), F32)],
        compiler_params=_params(("arbitrary", "arbitrary")),
        name="out_projection",
    )(oa, of, oc, w_bf16, x, g1, sh2, sc2, lng, lnb, wr_t)


def _ffn_body(x_ref, wg_ref, wu_ref, wd_ref, gate_ref, o_ref):
    f = pl.program_id(2)
    x = x_ref[0]
    a = jnp.dot(x, wg_ref[0].astype(BF16), preferred_element_type=F32)
    up = jnp.dot(x, wu_ref[0].astype(BF16), preferred_element_type=F32)
    h = (_silu(a) * up).astype(BF16)
    y = jnp.dot(h, wd_ref[0].astype(BF16), preferred_element_type=F32)

    @pl.when(f == 0)
    def _():
        o_ref[0] = y

    @pl.when(f > 0)
    def _():
        o_ref[0] += y

    @pl.when(f == pl.num_programs(2) - 1)
    def _():
        o_ref[0] *= gate_ref[0]


def _expert_ffn(xg, w_gate, w_up, w_down, gate):
    e, m, d = xg.shape
    ff = w_gate.shape[-1]
    tm = min(m, 2048)
    tf = 512
    return pl.pallas_call(
        _ffn_body,
        grid=(e, m // tm, ff // tf),
        in_specs=[pl.BlockSpec((1, tm, d), lambda i, j, k: (i, j, 0)),
                  pl.BlockSpec((1, d, tf), lambda i, j, k: (i, 0, k)),
                  pl.BlockSpec((1, d, tf), lambda i, j, k: (i, 0, k)),
                  pl.BlockSpec((1, tf, d), lambda i, j, k: (i, k, 0)),
                  pl.BlockSpec((1, tm, 1), lambda i, j, k: (i, j, 0))],
        out_specs=pl.BlockSpec((1, tm, d), lambda i, j, k: (i, j, 0)),
        out_shape=jax.ShapeDtypeStruct((e, m, d), F32),
        compiler_params=_params(("arbitrary", "arbitrary", "arbitrary")),
        name="expert_ffn",
    )(xg, w_gate, w_up, w_down, gate)


def _route_body(aff_ref, tri_ref, ones_ref, group_ref, lower_ref, idx_ref, gate_ref, cum_ref, *, cap):
    n_exp, nt, _ = cum_ref.shape
    rows = n_exp * nt
    a = aff_ref[0].reshape(rows, LANES)

    def as_bf16(mask):
        return jnp.where(mask, 1.0, 0.0).astype(BF16)

    def expert_total(mask):
        per_row = jnp.dot(as_bf16(mask), ones_ref[...], preferred_element_type=F32)
        return jnp.dot(group_ref[...], per_row.astype(BF16), preferred_element_type=F32)

    def token_prefix(mask):
        in_row = jnp.dot(as_bf16(mask), tri_ref[...], preferred_element_type=F32)
        row_total = jnp.broadcast_to(in_row[:, LANES - 1:LANES], in_row.shape)
        return in_row + jnp.dot(lower_ref[...], row_total.astype(BF16), preferred_element_type=F32)

    def enough(cand):
        return expert_total(a >= cand) >= cap

    power = jnp.ones((rows, LANES), F32)
    for k in range(6, -1, -1):
        too_few = jnp.logical_not(enough(power * 2.0 ** -(2 ** k - 1)))
        power = jnp.where(too_few, power * 2.0 ** -(2 ** k), power)
    base = jnp.where(enough(power), power, 0.0)

    def refine(_, carry):
        thr, digit = carry
        digit = digit * 0.5
        cand = thr + digit
        return jnp.where(enough(cand), cand, thr), digit

    thr, _ = lax.fori_loop(0, 23, refine, (base, base))
    above = a > thr
    tied = a == thr
    need = cap - expert_total(above)
    chosen = above | (tied & (token_prefix(tied) <= need))
    cum_ref[...] = token_prefix(chosen).reshape(n_exp, nt, LANES)

    slot = lax.broadcasted_iota(jnp.int32, (cap, 1), 0).astype(F32)
    tile_eye = lax.broadcasted_iota(jnp.int32, (nt, nt), 0) == lax.broadcasted_iota(jnp.int32, (nt, nt), 1)
    tile_id = lax.broadcasted_iota(jnp.int32, (cap, nt), 1).astype(F32)
    lane_id = lax.broadcasted_iota(jnp.int32, (cap, LANES), 1).astype(F32)

    def per_expert(e, carry):
        cum = cum_ref[e]
        aff = aff_ref[0, e]
        tile_end = jnp.sum(jnp.where(tile_eye, cum[:, LANES - 1:LANES], 0.0), axis=0, keepdims=True)
        tile_of_slot = jnp.sum(jnp.where(tile_end <= slot, 1.0, 0.0), axis=1, keepdims=True)
        pick_tile = jnp.where(tile_id == tile_of_slot, 1.0, 0.0).astype(BF16)
        cum_hi = jnp.floor(cum * (1.0 / 16.0))
        g1 = aff.astype(BF16)
        r1 = aff - g1.astype(F32)
        g2 = r1.astype(BF16)
        g3 = (r1 - g2.astype(F32)).astype(BF16)
        table = jnp.concatenate([cum_hi.astype(BF16), (cum - 16.0 * cum_hi).astype(BF16), g1, g2, g3], axis=1)
        got = jnp.dot(pick_tile, table, preferred_element_type=F32)
        counts = 16.0 * got[:, :LANES] + got[:, LANES:2 * LANES]
        within = jnp.sum(jnp.where(counts <= slot, 1.0, 0.0), axis=1, keepdims=True)
        affs = (got[:, 2 * LANES:3 * LANES] + got[:, 3 * LANES:4 * LANES]) + got[:, 4 * LANES:]
        idx_ref[0, e] = (LANES * tile_of_slot + within).astype(jnp.int32)
        gate_ref[0, e] = jnp.sum(jnp.where(lane_id == within, affs, 0.0), axis=1, keepdims=True)
        return carry

    lax.fori_loop(0, n_exp, per_expert, 0)


def _route(aff, cap):
    b, n_exp, nt, _ = aff.shape
    rows = n_exp * nt
    r = np.arange(rows)
    same_expert = (r[:, None] // nt) == (r[None, :] // nt)
    lane = np.arange(LANES)
    tri = jnp.asarray((lane[:, None] <= lane[None, :]).astype(np.float32), dtype=BF16)
    ones = jnp.ones((LANES, LANES), BF16)
    group = jnp.asarray(same_expert.astype(np.float32), dtype=BF16)
    lower = jnp.asarray((same_expert & (r[None, :] < r[:, None])).astype(np.float32), dtype=BF16)
    const = lambda i: (0, 0)
    return pl.pallas_call(
        functools.partial(_route_body, cap=cap),
        grid=(b,),
        in_specs=[pl.BlockSpec((1, n_exp, nt, LANES), lambda i: (i, 0, 0, 0)),
                  pl.BlockSpec((LANES, LANES), const),
                  pl.BlockSpec((LANES, LANES), const),
                  pl.BlockSpec((rows, rows), const),
                  pl.BlockSpec((rows, rows), const)],
        out_specs=[pl.BlockSpec((1, n_exp, cap, 1), lambda i: (i, 0, 0, 0)),
                   pl.BlockSpec((1, n_exp, cap, 1), lambda i: (i, 0, 0, 0))],
        out_shape=[jax.ShapeDtypeStruct((b, n_exp, cap, 1), jnp.int32),
                   jax.ShapeDtypeStruct((b, n_exp, cap, 1), F32)],
        scratch_shapes=[pltpu.VMEM((n_exp, nt, LANES), F32)],
        compiler_params=_params(("arbitrary",)),
        name="route",
    )(aff, tri, ones, group, lower)


ROW_UNROLL = 8


def _gather_body(idx_ref, h_ref, o_ref, rows_ref, *, cap):
    base = (pl.program_id(0) * pl.num_programs(1) + pl.program_id(1)) * cap

    def body(i, carry):
        for r in range(ROW_UNROLL):
            s = i * ROW_UNROLL + r
            rows_ref[pl.ds(s, 1), :] = h_ref[0, pl.ds(idx_ref[base + s], 1), :]
        return carry

    lax.fori_loop(0, cap // ROW_UNROLL, body, 0)
    o_ref[0] = rows_ref[...].astype(o_ref.dtype)


def _gather_rows(idx_flat, h, cap):
    b, t, d = h.shape
    return pl.pallas_call(
        functools.partial(_gather_body, cap=cap),
        grid_spec=pltpu.PrefetchScalarGridSpec(
            num_scalar_prefetch=1,
            grid=(b, N_EXPERTS),
            in_specs=[pl.BlockSpec((1, t, d), lambda i, e, idx: (i, 0, 0))],
            out_specs=pl.BlockSpec((1, cap, d), lambda i, e, idx: (e, i, 0)),
            scratch_shapes=[pltpu.VMEM((cap, d), F32)]),
        out_shape=jax.ShapeDtypeStruct((N_EXPERTS, b * cap, d), BF16),
        compiler_params=_params(("arbitrary", "arbitrary")),
        name="gather_rows",
    )(idx_flat, h)


def _combine_body(idx_ref, y_ref, o_ref, *, cap):
    e = pl.program_id(1)
    base = (pl.program_id(0) * pl.num_programs(1) + e) * cap

    @pl.when(e == 0)
    def _():
        o_ref[...] = jnp.zeros_like(o_ref)

    def body(i, carry):
        toks = [idx_ref[base + i * ROW_UNROLL + r] for r in range(ROW_UNROLL)]
        sums = [o_ref[0, pl.ds(toks[r], 1), :] + y_ref[0, pl.ds(i * ROW_UNROLL + r, 1), :] for r in range(ROW_UNROLL)]
        for r in range(ROW_UNROLL):
            o_ref[0, pl.ds(toks[r], 1), :] = sums[r]
        return carry

    lax.fori_loop(0, cap // ROW_UNROLL, body, 0)


def _combine_rows(idx_flat, y, b, t, cap):
    d = y.shape[-1]
    return pl.pallas_call(
        functools.partial(_combine_body, cap=cap),
        grid_spec=pltpu.PrefetchScalarGridSpec(
            num_scalar_prefetch=1,
            grid=(b, N_EXPERTS),
            in_specs=[pl.BlockSpec((1, cap, d), lambda i, e, idx: (e, i, 0))],
            out_specs=pl.BlockSpec((1, t, d), lambda i, e, idx: (i, 0, 0))),
        out_shape=jax.ShapeDtypeStruct((b, t, d), F32),
        compiler_params=_params(("arbitrary", "arbitrary")),
        name="combine_rows",
    )(idx_flat, y)


def _postnorm_body(x_ref, y_ref, g_ref, lng_ref, lnb_ref, o_ref):
    o_ref[0] = _layer_norm(DEEPNORM_ALPHA * x_ref[0] + g_ref[0] * y_ref[0]) * lng_ref[...] + lnb_ref[...]


def _post_norm(x, y, g, lng, lnb):
    b, t, d = x.shape
    tm = min(t, 1024)
    row = lambda i, j: (i, j, 0)
    return pl.pallas_call(
        _postnorm_body,
        grid=(b, t // tm),
        in_specs=[pl.BlockSpec((1, tm, d), row), pl.BlockSpec((1, tm, d), row),
                  pl.BlockSpec((1, 1, d), lambda i, j: (i, 0, 0)),
                  pl.BlockSpec((1, d), lambda i, j: (0, 0)), pl.BlockSpec((1, d), lambda i, j: (0, 0))],
        out_specs=pl.BlockSpec((1, tm, d), row),
        out_shape=jax.ShapeDtypeStruct((b, t, d), F32),
        compiler_params=_params(("arbitrary", "arbitrary")),
        name="post_norm",
    )(x, y, g, lng, lnb)


def _rope_tables(t):
    tok = jnp.arange(t, dtype=jnp.int32)
    lane = np.arange(LANES)
    d = lane % HEAD_DIM
    inv_freq = ROPE_BASE ** (-jnp.arange(0, HEAD_DIM // 2, 2, dtype=F32) / (HEAD_DIM // 2))
    freq = inv_freq[d % 16]
    use_col = jnp.asarray((d // 32) == 1)
    position = jnp.where(use_col[None, :], (tok % GRID_W)[:, None], (tok // GRID_W)[:, None])
    ang = position * freq[None, :]
    sign = jnp.asarray(np.where((d % 32) < 16, -1.0, 1.0).astype(np.float32))
    return jnp.cos(ang).astype(F32), (jnp.sin(ang) * sign[None, :]).astype(F32)


def _block_ones(width):
    idx = np.arange(width) // HEAD_DIM
    return jnp.asarray((idx[:, None] == idx[None, :]).astype(np.float32), dtype=BF16)


MIN_ROUTE_TILES = 16


def _moe(h2, aff, w_gate, w_up, w_down):
    b, t, _ = h2.shape
    cap = EC_CAPACITY_FACTOR * t // N_EXPERTS
    nt = t // LANES
    aff_tiles = aff.reshape(b, N_EXPERTS, nt, LANES)
    if nt < MIN_ROUTE_TILES:
        aff_tiles = jnp.pad(aff_tiles, ((0, 0), (0, 0), (0, MIN_ROUTE_TILES - nt), (0, 0)))
    idx, gate = _route(aff_tiles, cap)
    idx_flat = idx.reshape(-1)
    xg = _gather_rows(idx_flat, h2, cap)
    gate_e = jnp.swapaxes(gate.reshape(b, N_EXPERTS, cap), 0, 1).reshape(N_EXPERTS, b * cap, 1)
    y = _expert_ffn(xg, w_gate, w_up, w_down, gate_e)
    return _combine_rows(idx_flat, y, b, t, cap)


def kernel(x, c, ctx, c_ctx, w_mod, b_mod, w_in, q_norm_a, k_norm_a, w_fourier, b_fourier, sink_c, w_out, ln1_g,
           ln1_b, w_router, w_gate, w_up, w_down, ln2_g, ln2_b):
    b, t, d = x.shape
    cos, sin = _rope_tables(t)
    gq, gk = _block_ones(A_Q), _block_ones(A_KV)
    cc = jnp.zeros((MOD_ROWS, d), F32).at[:b].set(c).at[b].set(c_ctx)
    mod = _modulation(cc, w_mod, b_mod)
    w_in_bf = w_in.astype(BF16)
    w_out_bf = w_out.astype(BF16)

    x_lat, x_ctx = x, ctx
    for layer in range(DEPTH):
        update_ctx = layer < DEPTH - 1
        lat = [mod[layer, :b, i * d:(i + 1) * d][:, None, :] for i in range(6)]
        cm = [jnp.broadcast_to(mod[layer, b, i * d:(i + 1) * d][None, None, :], (b, 1, d)) for i in range(6)]
        qn = jnp.tile(q_norm_a[layer], A_HEADS)[None, :]
        kn = jnp.tile(k_norm_a[layer], A_KV_HEADS)[None, :]
        w_four = jax.scipy.linalg.block_diag(*[w_fourier[layer, g] for g in range(B_GROUPS)])
        bias = b_fourier[layer].reshape(1, B_W)
        lng1, lnb1 = ln1_g[layer][None, :], ln1_b[layer][None, :]
        lng2, lnb2 = ln2_g[layer][None, :], ln2_b[layer][None, :]
        wr_t = w_router[layer].T
        sink = sink_c[layer]

        proj_ctx = _in_projection(x_ctx, cm[0], cm[1], w_in_bf[layer], qn, kn, gq, gk, cos, sin, rope=False)
        qa_c, qc_c, u_c, ka_c, va_c, kc_c, vc_c = proj_ctx
        qa, qc, u, ka, va, kc, vc = _in_projection(x_lat, lat[0], lat[1], w_in_bf[layer], qn, kn, gq, gk, cos, sin,
                                                   rope=True)
        out_a = _attention(qa, ka_c, va_c, ka, va, heads_per_kv=4, mode="full")
        out_b = _fourier_mix(u, w_four, bias)
        out_c = _attention(qc, kc_c, vc_c, kc, vc, sink, heads_per_kv=2, mode="window")
        x1, h2, aff = _out_projection(out_a, out_b, out_c, w_out_bf[layer], x_lat, lat[2], lat[3], lat[4],
                                      lng1, lnb1, wr_t)
        y = _moe(h2, aff, w_gate[layer], w_up[layer], w_down[layer])
        x_lat = _post_norm(x1, y, lat[5], lng2, lnb2)

        if update_ctx:
            out_a_c = _attention(qa_c, ka_c, va_c, heads_per_kv=4, mode="none")
            out_b_c = _fourier_mix(u_c, w_four, bias)
            out_c_c = _attention(qc_c, kc_c, vc_c, sink=sink, heads_per_kv=2, mode="none")
            x1c, h2c, affc = _out_projection(out_a_c, out_b_c, out_c_c, w_out_bf[layer], x_ctx, cm[2], cm[3], cm[4],
                                             lng1, lnb1, wr_t)
            yc = _moe(h2c, affc, w_gate[layer], w_up[layer], w_down[layer])
            x_ctx = _post_norm(x1c, yc, cm[5], lng2, lnb2)
    return x_lat
```

```python
import functools

import numpy as np
import jax
import jax.numpy as jnp
from jax import lax
from jax.experimental import pallas as pl
from jax.experimental.pallas import tpu as pltpu

F32 = jnp.float32
BF16 = jnp.bfloat16

D_MODEL = 1024
DEPTH = 2
GRID_W = 64
HEAD_DIM = 64
ROPE_BASE = 10000.0
A_HEADS, A_KV_HEADS = 8, 2
B_GROUPS, B_GROUP_DIM = 4, 64
C_HEADS, C_KV_HEADS = 4, 2
WINDOW = 128
N_EXPERTS = 16
EC_CAPACITY_FACTOR = 2
EXPERT_FF = 2 * D_MODEL
A_Q, A_KV, B_W, C_Q, C_KV = 512, 128, 256, 256, 128
QU_WIDTH = A_Q + C_Q + B_W
IN_WIDTH = QU_WIDTH + 2 * A_KV + 2 * C_KV
LN_EPS = 1e-5
RMS_EPS = 1e-6
NEG_INF = -1e30
LOG2E = 1.4426950408889634
SAFE_LOGIT = 60.0
DEEPNORM_ALPHA = (2 * DEPTH) ** 0.25
MOD_ROWS = 16
LANES = 128
VMEM_LIMIT = 56 * 1024 * 1024

HIGHEST = lax.Precision.HIGHEST
NT_DIMS = (((1,), (1,)), ((), ()))


def _params(sem):
    return pltpu.CompilerParams(dimension_semantics=sem, vmem_limit_bytes=VMEM_LIMIT)


def _layer_norm(x):
    mu = jnp.mean(x, axis=-1, keepdims=True)
    xc = x - mu
    var = jnp.mean(xc * xc, axis=-1, keepdims=True)
    return xc * lax.rsqrt(var + LN_EPS)


def _silu(x):
    return x * (1.0 / (1.0 + jnp.exp(-x)))


def _mod_body(c_ref, w_ref, b_ref, o_ref):
    s = _silu(c_ref[...])
    o_ref[0] = jnp.dot(s, w_ref[0], precision=HIGHEST, preferred_element_type=F32) + b_ref[0]


def _modulation(cc, w_mod, b_mod):
    tn = 1024
    n = w_mod.shape[-1]
    return pl.pallas_call(
        _mod_body,
        grid=(DEPTH, n // tn),
        in_specs=[
            pl.BlockSpec((MOD_ROWS, D_MODEL), lambda l, j: (0, 0)),
            pl.BlockSpec((1, D_MODEL, tn), lambda l, j: (l, 0, j)),
            pl.BlockSpec((1, 1, tn), lambda l, j: (l, 0, j)),
        ],
        out_specs=pl.BlockSpec((1, MOD_ROWS, tn), lambda l, j: (l, 0, j)),
        out_shape=jax.ShapeDtypeStruct((DEPTH, MOD_ROWS, n), F32),
        compiler_params=_params(("arbitrary", "arbitrary")),
        name="modulation",
    )(cc, w_mod, b_mod.reshape(DEPTH, 1, n))


def _rope(x, cos, sin_signed, first_half):
    outs = []
    for j in range(x.shape[1] // LANES):
        xj = x[:, LANES * j:LANES * (j + 1)]
        partner = jnp.where(first_half, pltpu.roll(xj, LANES - 16, 1), pltpu.roll(xj, 16, 1))
        outs.append(xj * cos + partner * sin_signed)
    return jnp.concatenate(outs, axis=1) if len(outs) > 1 else outs[0]


def _dup_kv_heads(k, copies):
    lane = lax.broadcasted_iota(jnp.int32, (1, LANES), 1)
    lo = lane < HEAD_DIM
    r = pltpu.roll(k, HEAD_DIM, 1)
    d0 = jnp.where(lo, k, r)
    d1 = jnp.where(lo, r, k)
    reps = copies // 2
    return jnp.concatenate([d0] * reps + [d1] * reps, axis=1)


def _inproj_body(x_ref, sh_ref, sc_ref, w_ref, qn_ref, kn_ref, gq_ref, gk_ref, cos_ref, sin_ref,
                 qa_ref, qc_ref, u_ref, ka_ref, va_ref, kc_ref, vc_ref, *, rope):
    h = _layer_norm(x_ref[0]) * (1.0 + sc_ref[0]) + sh_ref[0]
    p = jnp.dot(h.astype(BF16), w_ref[...], preferred_element_type=F32)
    qa = p[:, :A_Q]
    qc = p[:, A_Q:A_Q + C_Q]
    u = p[:, A_Q + C_Q:QU_WIDTH]
    ka = p[:, QU_WIDTH:QU_WIDTH + A_KV]
    va = p[:, QU_WIDTH + A_KV:QU_WIDTH + 2 * A_KV]
    kc = p[:, QU_WIDTH + 2 * A_KV:QU_WIDTH + 2 * A_KV + C_KV]
    vc = p[:, QU_WIDTH + 2 * A_KV + C_KV:]
    msq = jnp.dot((qa * qa).astype(BF16), gq_ref[...], preferred_element_type=F32) * (1.0 / HEAD_DIM)
    qa = qa * lax.rsqrt(msq + RMS_EPS) * qn_ref[...]
    msk = jnp.dot((ka * ka).astype(BF16), gk_ref[...], preferred_element_type=F32) * (1.0 / HEAD_DIM)
    ka = ka * lax.rsqrt(msk + RMS_EPS) * kn_ref[...]
    if rope:
        cos = cos_ref[...]
        sin = sin_ref[...]
        lane = lax.broadcasted_iota(jnp.int32, (1, LANES), 1)
        first_half = (lane % 32) < 16
        qa = _rope(qa, cos, sin, first_half)
        qc = _rope(qc, cos, sin, first_half)
        ka = _rope(ka, cos, sin, first_half)
        kc = _rope(kc, cos, sin, first_half)
    scale = HEAD_DIM ** -0.5 * LOG2E
    qa_ref[0] = (qa * scale).astype(BF16)
    qc_ref[0] = (qc * scale).astype(BF16)
    u_ref[0] = u
    ka_ref[0] = _dup_kv_heads(ka, A_HEADS // A_KV_HEADS).astype(BF16)
    va_ref[0] = _dup_kv_heads(va, A_HEADS // A_KV_HEADS).astype(BF16)
    kc_ref[0] = _dup_kv_heads(kc, C_HEADS // C_KV_HEADS).astype(BF16)
    vc_ref[0] = _dup_kv_heads(vc, C_HEADS // C_KV_HEADS).astype(BF16)


def _in_projection(x, sh, sc, w_bf16, qn, kn, gq, gk, cos, sin, *, rope):
    b, t, d = x.shape
    tm = min(t, 512)
    row = lambda i, j: (i, j, 0)
    per_b = lambda i, j: (i, 0, 0)
    const = lambda i, j: (0, 0)
    tab = (lambda i, j: (j, 0)) if rope else const
    outs = [(A_Q, BF16), (C_Q, BF16), (B_W, F32), (4 * A_KV, BF16), (4 * A_KV, BF16),
            (2 * C_KV, BF16), (2 * C_KV, BF16)]
    return pl.pallas_call(
        functools.partial(_inproj_body, rope=rope),
        grid=(b, t // tm),
        in_specs=[
            pl.BlockSpec((1, tm, d), row),
            pl.BlockSpec((1, 1, d), per_b),
            pl.BlockSpec((1, 1, d), per_b),
            pl.BlockSpec((d, IN_WIDTH), const),
            pl.BlockSpec((1, A_Q), const),
            pl.BlockSpec((1, A_KV), const),
            pl.BlockSpec((A_Q, A_Q), const),
            pl.BlockSpec((A_KV, A_KV), const),
            pl.BlockSpec((tm, LANES), tab),
            pl.BlockSpec((tm, LANES), tab),
        ],
        out_specs=[pl.BlockSpec((1, tm, w), row) for w, _ in outs],
        out_shape=[jax.ShapeDtypeStruct((b, t, w), dt) for w, dt in outs],
        compiler_params=_params(("arbitrary", "arbitrary")),
        name="in_projection",
    )(x, sh, sc, w_bf16, qn, kn, gq, gk, cos, sin)


def _attn_body(*refs, heads_per_kv, mode, has_sink, tq, t_lat, kc_full):
    refs = list(refs)
    q_ref, kctx_ref, vctx_ref = refs[:3]
    pos = 3
    klat_ref = vlat_ref = sink_ref = None
    if mode != "none":
        klat_ref, vlat_ref = refs[pos:pos + 2]
        pos += 2
    if has_sink:
        sink_ref = refs[pos]
        pos += 1
    o_ref, kmax_ref = refs[pos:pos + 2]

    width = heads_per_kv * HEAD_DIM
    kvh = pl.program_id(1)
    qi = pl.program_id(2)
    q = q_ref[0]
    lane = lax.broadcasted_iota(jnp.int32, (1, width), 1)
    head_lanes = [(lane // HEAD_DIM) == g for g in range(heads_per_kv)]
    q_stack = jnp.concatenate([jnp.where(hl, q, jnp.zeros_like(q)) for hl in head_lanes], axis=0)
    sinks =[sink_ref[kvh * heads_per_kv + g] * LOG2E for g in range(heads_per_kv)] if has_sink else None

    def per_head_lanes(cols):
        full = jnp.broadcast_to(cols[0], (tq, width))
        for g in range(1, heads_per_kv):
            full = jnp.where(head_lanes[g], cols[g], full)
        return full

    def key_sq_norm_max(ref):
        rows = ref.shape[1]
        step_rows = min(rows, 1024)

        def body(c, best):
            kf = ref[0, pl.ds(pl.multiple_of(c * step_rows, step_rows), step_rows), :].astype(F32)
            return jnp.maximum(best, jnp.max(jnp.sum(kf * kf, axis=1, keepdims=True), axis=0, keepdims=True))
        best = lax.fori_loop(0, rows // step_rows, body, jnp.zeros((1, 1), F32))
        return best[0, 0] * (1.0 / heads_per_kv)

    @pl.when(qi == 0)
    def _():
        best = key_sq_norm_max(kctx_ref)
        if mode != "none":
            best = jnp.maximum(best, key_sq_norm_max(klat_ref))
        kmax_ref[0] = best

    def sweep(stabilised):
        def by_head(stacked):
            return [stacked[g * tq:(g + 1) * tq] for g in range(heads_per_kv)]

        def step(carry, k, v, mask=None):
            m, l, acc = carry
            s = lax.dot_general(q_stack, k, NT_DIMS, preferred_element_type=F32)
            if mask is not None:
                s = jnp.where(jnp.concatenate([mask] * heads_per_kv, axis=0), s, NEG_INF)
            if stabilised:
                m_new = jnp.maximum(m, jnp.max(s, axis=1, keepdims=True))
                alpha = jnp.exp2(m - m_new)
                p = jnp.exp2(s - m_new)
                l = alpha * l + jnp.sum(p, axis=1, keepdims=True)
                acc = acc * per_head_lanes(by_head(alpha))
            else:
                m_new = m
                p = jnp.exp2(s)
                l = l + jnp.sum(p, axis=1, keepdims=True)
            pv = jnp.dot(p.astype(BF16), v, preferred_element_type=F32)
            for g, part in enumerate(by_head(pv)):
                acc = acc + jnp.where(head_lanes[g], part, 0.0)
            return m_new, l, acc

        rows_of = lambda vals: jnp.concatenate([jnp.full((tq, 1), v, F32) for v in vals], axis=0)
        if not has_sink:
            m0 = jnp.full((heads_per_kv * tq, 1), NEG_INF if stabilised else 0.0, F32)
            l0 = jnp.zeros((heads_per_kv * tq, 1), F32)
        elif stabilised:
            m0 = rows_of(sinks)
            l0 = jnp.ones((heads_per_kv * tq, 1), F32)
        else:
            m0 = jnp.zeros((heads_per_kv * tq, 1), F32)
            l0 = jnp.exp2(rows_of(sinks))
        carry = (m0, l0, jnp.zeros((tq, width), F32))
        carry = step(carry, kctx_ref[0], vctx_ref[0])
        if mode == "full":
            for c in range(t_lat // kc_full):
                rows = pl.ds(c * kc_full, kc_full)
                carry = step(carry, klat_ref[0, rows, :], vlat_ref[0, rows, :])
        elif mode == "window":
            span = tq + 2 * WINDOW
            q0 = qi * tq
            k0 = pl.multiple_of(jnp.clip(q0 - WINDOW, 0, t_lat - span), WINDOW)
            qpos = q0 + lax.broadcasted_iota(jnp.int32, (tq, 1), 0)
            kpos = k0 + lax.broadcasted_iota(jnp.int32, (1, span), 1)
            in_window = jnp.abs(kpos - qpos) <= WINDOW
            carry = step(carry, klat_ref[0, pl.ds(k0, span), :], vlat_ref[0, pl.ds(k0, span), :], in_window)
        _, l, acc = carry
        return acc * per_head_lanes(by_head(1.0 / l))

    qf = q.astype(F32)
    qf = qf * qf
    head_sq = [jnp.sum(jnp.where(hl, qf, 0.0), axis=1, keepdims=True) for hl in head_lanes]
    q_sq = jnp.max(functools.reduce(jnp.maximum, head_sq), axis=0, keepdims=True)[0, 0]
    bounded = q_sq * kmax_ref[0] <= SAFE_LOGIT * SAFE_LOGIT
    if has_sink:
        for g in range(heads_per_kv):
            bounded = jnp.logical_and(bounded, jnp.abs(sinks[g]) <= SAFE_LOGIT)
    out = lax.cond(bounded, lambda: sweep(False), lambda: sweep(True))
    o_ref[0] = out.astype(o_ref.dtype)


def _attention(q, kctx, vctx, klat=None, vlat=None, sink=None, *, heads_per_kv, mode):
    b, t, _ = q.shape
    n_kv = 2
    width = heads_per_kv * HEAD_DIM
    t_ctx = kctx.shape[1]
    tq = min(t, 256)
    qmap = lambda i, h, j: (i, j, h)
    kmap = lambda i, h, j: (i, 0, h)
    in_specs = [pl.BlockSpec((1, tq, width), qmap),
                pl.BlockSpec((1, t_ctx, width), kmap),
                pl.BlockSpec((1, t_ctx, width), kmap)]
    args = [q, kctx, vctx]
    t_lat = 0
    if mode != "none":
        t_lat = klat.shape[1]
        in_specs += [pl.BlockSpec((1, t_lat, width), kmap), pl.BlockSpec((1, t_lat, width), kmap)]
        args += [klat, vlat]
    if sink is not None:
        in_specs.append(pl.BlockSpec(memory_space=pltpu.SMEM))
        args.append(sink)
    body = functools.partial(_attn_body, heads_per_kv=heads_per_kv, mode=mode, has_sink=sink is not None,
                             tq=tq, t_lat=t_lat, kc_full=min(max(t_lat, 1), 1024))
    return pl.pallas_call(
        body,
        grid=(b, n_kv, t // tq),
        in_specs=in_specs,
        out_specs=pl.BlockSpec((1, tq, width), qmap),
        out_shape=jax.ShapeDtypeStruct((b, t, n_kv * width), BF16),
        scratch_shapes=[pltpu.SMEM((1,), F32)],
        compiler_params=_params(("arbitrary", "arbitrary", "arbitrary")),
        name=f"attention_{mode}_g{heads_per_kv}",
    )(*args)


def _four_chan_body(u_ref, wblk_ref, cc_ref, sc_ref, z_ref):
    w = wblk_ref[...]
    wc = jnp.dot(cc_ref[...], w, precision=HIGHEST, preferred_element_type=F32)
    ws = jnp.dot(sc_ref[...], w, precision=HIGHEST, preferred_element_type=F32)
    u = u_ref[0]
    z_ref[0, 0] = jnp.dot(u, wc, precision=HIGHEST, preferred_element_type=F32)
    z_ref[0, 1] = -jnp.dot(u, ws, precision=HIGHEST, preferred_element_type=F32)


def _four_rows_body(z_ref, m_ref, tc_ref, ts_ref, a_ref):
    n1 = z_ref.shape[2]
    z = jnp.concatenate([z_ref[0, 0], z_ref[0, 1]], axis=0)
    a = jnp.dot(m_ref[...], z, precision=HIGHEST, preferred_element_type=F32)
    ar, ai = a[:n1], a[n1:]
    tc, ts = tc_ref[...], ts_ref[...]
    a_ref[0, 0] = ar * tc + ai * ts
    a_ref[0, 1] = ai * tc - ar * ts


def _four_cols_body(a_ref, cs_ref, bias_ref, y_ref):
    kb = a_ref.shape[2]
    for j in range(kb):
        rhs = jnp.concatenate([a_ref[0, 0, j], a_ref[0, 1, j]], axis=0)
        res = jnp.dot(cs_ref[...], rhs, precision=HIGHEST, preferred_element_type=F32)
        y_ref[0, :, j, :] = res + bias_ref[...]


def _dft_constants(t):
    n = int(round(t ** 0.5))
    assert n * n == t
    k = np.arange(n)
    ang = 2.0 * np.pi * np.outer(k, k) / n
    c, s = np.cos(ang), np.sin(ang)
    rows = np.block([[c, s], [-s, c]]).astype(np.float32)
    cols = np.concatenate([c, s], axis=1).astype(np.float32)
    tw = 2.0 * np.pi * np.outer(k, k) / t
    tc = np.repeat(np.cos(tw), B_W, axis=1).astype(np.float32)
    ts = np.repeat(np.sin(tw), B_W, axis=1).astype(np.float32)
    return n, rows, cols, tc, ts


def _channel_constants(t):
    k = np.arange(B_GROUP_DIM)
    ang = 2.0 * np.pi * np.outer(k, k) / B_GROUP_DIM
    norm = 1.0 / np.sqrt(float(t) * B_GROUP_DIM)
    eye = np.eye(B_GROUPS)
    cc = np.kron(eye, np.cos(ang) * norm).astype(np.float32)
    sc = np.kron(eye, np.sin(ang) * norm).astype(np.float32)
    return cc, sc


def _fourier_mix(u, w_blockdiag, bias):
    b, t, _ = u.shape
    n, rows, cols, tc, ts = _dft_constants(t)
    cc, sc = _channel_constants(t)
    tm = min(t, 1024)
    z = pl.pallas_call(
        _four_chan_body,
        grid=(b, t // tm),
        in_specs=[pl.BlockSpec((1, tm, B_W), lambda i, j: (i, j, 0)),
                  pl.BlockSpec((B_W, B_W), lambda i, j: (0, 0)),
                  pl.BlockSpec((B_W, B_W), lambda i, j: (0, 0)),
                  pl.BlockSpec((B_W, B_W), lambda i, j: (0, 0))],
        out_specs=pl.BlockSpec((1, 2, tm, B_W), lambda i, j: (i, 0, j, 0)),
        out_shape=jax.ShapeDtypeStruct((b, 2, t, B_W), F32),
        compiler_params=_params(("arbitrary", "arbitrary")),
        name="fourier_channels",
    )(u, w_blockdiag, jnp.asarray(cc), jnp.asarray(sc))
    cols_total = n * B_W
    tc_cols = min(cols_total, 4096)
    a = pl.pallas_call(
        _four_rows_body,
        grid=(b, cols_total // tc_cols),
        in_specs=[pl.BlockSpec((1, 2, n, tc_cols), lambda i, j: (i, 0, 0, j)),
                  pl.BlockSpec((2 * n, 2 * n), lambda i, j: (0, 0)),
                  pl.BlockSpec((n, tc_cols), lambda i, j: (0, j)),
                  pl.BlockSpec((n, tc_cols), lambda i, j: (0, j))],
        out_specs=pl.BlockSpec((1, 2, n, tc_cols), lambda i, j: (i, 0, 0, j)),
        out_shape=jax.ShapeDtypeStruct((b, 2, n, cols_total), F32),
        compiler_params=_params(("arbitrary", "arbitrary")),
        name="fourier_rows",
    )(z.reshape(b, 2, n, cols_total), jnp.asarray(rows), jnp.asarray(tc), jnp.asarray(ts))
    kb = 8
    y = pl.pallas_call(
        _four_cols_body,
        grid=(b, n // kb),
        in_specs=[pl.BlockSpec((1, 2, kb, n, B_W), lambda i, j: (i, 0, j, 0, 0)),
                  pl.BlockSpec((n, 2 * n), lambda i, j: (0, 0)),
                  pl.BlockSpec((1, B_W), lambda i, j: (0, 0))],
        out_specs=pl.BlockSpec((1, n, kb, B_W), lambda i, j: (i, 0, j, 0)),
        out_shape=jax.ShapeDtypeStruct((b, n, n, B_W), F32),
        compiler_params=_params(("arbitrary", "arbitrary")),
        name="fourier_cols",
    )(a.reshape(b, 2, n, n, B_W), jnp.asarray(cols), bias)
    return y.reshape(b, t, B_W)


def _outproj_body(a_ref, f_ref, c_ref, w_ref, x_ref, g1_ref, sh2_ref, sc2_ref, lng_ref, lnb_ref, wr_ref,
                  x1_ref, h2_ref, aff_ref):
    o = jnp.dot(a_ref[0], w_ref[:A_Q], preferred_element_type=F32)
    o = o + jnp.dot(f_ref[0].astype(BF16), w_ref[A_Q:A_Q + B_W], preferred_element_type=F32)
    o = o + jnp.dot(c_ref[0], w_ref[A_Q + B_W:], preferred_element_type=F32)
    x1 = _layer_norm(DEEPNORM_ALPHA * x_ref[0] + g1_ref[0] * o) * lng_ref[...] + lnb_ref[...]
    x1_ref[0] = x1
    h2 = _layer_norm(x1) * (1.0 + sc2_ref[0]) + sh2_ref[0]
    h2_ref[0] = h2
    logits = lax.dot_general(wr_ref[...], h2, NT_DIMS, precision=HIGHEST, preferred_element_type=F32)
    e = jnp.exp(logits - jnp.max(logits, axis=0, keepdims=True))
    aff_ref[0] = e / jnp.sum(e, axis=0, keepdims=True)


def _out_projection(oa, of, oc, w_bf16, x, g1, sh2, sc2, lng, lnb, wr_t):
    b, t, d = x.shape
    tm = min(t, 512)
    row = lambda i, j: (i, j, 0)
    per_b = lambda i, j: (i, 0, 0)
    const = lambda i, j: (0, 0)
    return pl.pallas_call(
        _outproj_body,
        grid=(b, t // tm),
        in_specs=[pl.BlockSpec((1, tm, A_Q), row),
                  pl.BlockSpec((1, tm, B_W), row),
                  pl.BlockSpec((1, tm, C_Q), row),
                  pl.BlockSpec((d, d), const),
                  pl.BlockSpec((1, tm, d), row),
                  pl.BlockSpec((1, 1, d), per_b),
                  pl.BlockSpec((1, 1, d), per_b),
                  pl.BlockSpec((1, 1, d), per_b),
                  pl.BlockSpec((1, d), const),
                  pl.BlockSpec((1, d), const),
                  pl.BlockSpec((N_EXPERTS, d), const)],
        out_specs=[pl.BlockSpec((1, tm, d), row),
                   pl.BlockSpec((1, tm, d), row),
                   pl.BlockSpec((1, N_EXPERTS, tm), lambda i, j: (i, 0, j))],
        out_shape=[jax.ShapeDtypeStruct((b, t, d), F32),
                   jax.ShapeDtypeStruct((b, t, d), F32),
                   jax.ShapeDtypeStruct((b, N_EXPERTS, t), F32)],
        compiler_params=_params(("arbitrary", "arbitrary")),
        name="out_projection",
    )(oa, of, oc, w_bf16, x, g1, sh2, sc2, lng, lnb, wr_t)


def _ffn_body(x_ref, wg_ref, wu_ref, wd_ref, gate_ref, o_ref):
    @pl.when(pl.program_id(2) == 0)
    def _():
        o_ref[...] = jnp.zeros_like(o_ref)

    x = x_ref[0]
    a = jnp.dot(x, wg_ref[0, 0].astype(BF16), preferred_element_type=F32)
    up = jnp.dot(x, wu_ref[0, 0].astype(BF16), preferred_element_type=F32)
    h = (_silu(a) * up).astype(BF16)
    y = jnp.dot(h, wd_ref[0, 0].astype(BF16), preferred_element_type=F32)
    o_ref[0] += gate_ref[0] * y


def _expert_ffn(xg, w_gate, w_up, w_down, gate, layer):
    e, m, d = xg.shape
    ff = w_gate.shape[-1]
    tm = min(m, 2048)
    tf = 512
    return pl.pallas_call(
        _ffn_body,
        grid=(e, m // tm, ff // tf),
        in_specs=[pl.BlockSpec((1, tm, d), lambda i, j, k: (i, j, 0)),
                  pl.BlockSpec((1, 1, d, tf), lambda i, j, k: (layer, i, 0, k)),
                  pl.BlockSpec((1, 1, d, tf), lambda i, j, k: (layer, i, 0, k)),
                  pl.BlockSpec((1, 1, tf, d), lambda i, j, k: (layer, i, k, 0)),
                  pl.BlockSpec((1, tm, 1), lambda i, j, k: (i, j, 0))],
        out_specs=pl.BlockSpec((1, tm, d), lambda i, j, k: (i, j, 0)),
        out_shape=jax.ShapeDtypeStruct((e, m, d), F32),
        compiler_params=_params(("arbitrary", "arbitrary", "arbitrary")),
        name="expert_ffn",
    )(xg, w_gate, w_up, w_down, gate)


def _route_body(aff_ref, tri_ref, lower_ref, idx_ref, gate_ref, cum_ref, *, cap):
    nb, n_exp, nt, _ = aff_ref.shape
    lists = nb * n_exp
    rows = n_exp * nt
    a = aff_ref[...].reshape(lists, nt, LANES)

    def count(mask):
        ones = jnp.where(mask, 1.0, 0.0)
        return jnp.sum(jnp.sum(ones, axis=1, keepdims=True), axis=2, keepdims=True)

    def token_prefix(mask):
        flat = jnp.where(mask, 1.0, 0.0).astype(BF16).reshape(nb * rows, LANES)
        in_row = jnp.dot(flat, tri_ref[...], preferred_element_type=F32)
        row_total = jnp.broadcast_to(in_row[:, LANES - 1:LANES], in_row.shape).astype(BF16)
        before = [jnp.dot(lower_ref[...], row_total[i * rows:(i + 1) * rows], preferred_element_type=F32)
                  for i in range(nb)]
        return (in_row + jnp.concatenate(before, axis=0)).reshape(lists, nt, LANES)

    def enough(cand):
        return count(a >= cand) >= cap

    power = jnp.ones((lists, 1, 1), F32)
    for k in range(6, -1, -1):
        too_few = jnp.logical_not(enough(power * 2.0 ** -(2 ** k - 1)))
        power = jnp.where(too_few, power * 2.0 ** -(2 ** k), power)
    base = jnp.where(enough(power), power, 0.0)

    def refine(_, carry):
        thr, digit = carry
        digit = digit * 0.5
        cand = thr + digit
        return jnp.where(enough(cand), cand, thr), digit

    thr, _ = lax.fori_loop(0, 23, refine, (base, base))
    above = a > thr
    tied = a == thr
    need = cap - count(above)
    chosen = above | (tied & (token_prefix(tied) <= need))
    cum_ref[...] = token_prefix(chosen)

    slot = lax.broadcasted_iota(jnp.int32, (cap, 1), 0).astype(F32)
    tile_eye = lax.broadcasted_iota(jnp.int32, (nt, nt), 0) == lax.broadcasted_iota(jnp.int32, (nt, nt), 1)
    tile_id = lax.broadcasted_iota(jnp.int32, (cap, nt), 1).astype(F32)
    lane_id = lax.broadcasted_iota(jnp.int32, (cap, LANES), 1).astype(F32)

    blk = min(cap, LANES)
    blk_eye = lax.broadcasted_iota(jnp.int32, (blk, blk), 0) == lax.broadcasted_iota(jnp.int32, (blk, blk), 1)

    def as_row(col):
        parts = [jnp.sum(jnp.where(blk_eye, col[j * blk:(j + 1) * blk], 0.0), axis=0, keepdims=True)
                 for j in range(cap // blk)]
        return jnp.concatenate(parts, axis=1) if len(parts) > 1 else parts[0]

    def per_list(i, carry):
        cum = cum_ref[i]
        aff = aff_ref[i // n_exp, i % n_exp]
        tile_end = jnp.sum(jnp.where(tile_eye, cum[:, LANES - 1:LANES], 0.0), axis=0, keepdims=True)
        tile_of_slot = jnp.sum(jnp.where(tile_end <= slot, 1.0, 0.0), axis=1, keepdims=True)
        pick_tile = jnp.where(tile_id == tile_of_slot, 1.0, 0.0).astype(BF16)
        cum_hi = jnp.floor(cum * (1.0 / 16.0))
        g1 = aff.astype(BF16)
        r1 = aff - g1.astype(F32)
        g2 = r1.astype(BF16)
        g3 = (r1 - g2.astype(F32)).astype(BF16)
        table = jnp.concatenate([cum_hi.astype(BF16), (cum - 16.0 * cum_hi).astype(BF16), g1, g2, g3], axis=1)
        got = jnp.dot(pick_tile, table, preferred_element_type=F32)
        counts = 16.0 * got[:, :LANES] + got[:, LANES:2 * LANES]
        within = jnp.sum(jnp.where(counts <= slot, 1.0, 0.0), axis=1, keepdims=True)
        affs = (got[:, 2 * LANES:3 * LANES] + got[:, 3 * LANES:4 * LANES]) + got[:, 4 * LANES:]
        gate = jnp.sum(jnp.where(lane_id == within, affs, 0.0), axis=1, keepdims=True)
        idx_ref[pl.ds(i, 1), :] = as_row(LANES * tile_of_slot + within).astype(jnp.int32)
        gate_ref[pl.ds(i, 1), :] = as_row(gate)
        return carry

    lax.fori_loop(0, lists, per_list, 0)


def _route(aff, cap):
    b, n_exp, nt, _ = aff.shape
    rows = n_exp * nt
    r = np.arange(rows)
    same_expert = (r[:, None] // nt) == (r[None, :] // nt)
    lane = np.arange(LANES)
    tri = jnp.asarray((lane[:, None] <= lane[None, :]).astype(np.float32), dtype=BF16)
    lower = jnp.asarray((same_expert & (r[None, :] < r[:, None])).astype(np.float32), dtype=BF16)
    whole = lambda i: (0, 0, 0, 0)
    return pl.pallas_call(
        functools.partial(_route_body, cap=cap),
        grid=(1,),
        in_specs=[pl.BlockSpec((b, n_exp, nt, LANES), whole),
                  pl.BlockSpec((LANES, LANES), lambda i: (0, 0)),
                  pl.BlockSpec((rows, rows), lambda i: (0, 0))],
        out_specs=[pl.BlockSpec((b * n_exp, cap), lambda i: (0, 0)),
                   pl.BlockSpec((b * n_exp, cap), lambda i: (0, 0))],
        out_shape=[jax.ShapeDtypeStruct((b * n_exp, cap), jnp.int32),
                   jax.ShapeDtypeStruct((b * n_exp, cap), F32)],
        scratch_shapes=[pltpu.VMEM((b * n_exp, nt, LANES), F32)],
        compiler_params=_params(("arbitrary",)),
        name="route",
    )(aff, tri, lower)


ROW_UNROLL = 8


def _gather_body(idx_ref, h_ref, o_ref, rows_ref, *, cap):
    base = (pl.program_id(0) * pl.num_programs(1) + pl.program_id(1)) * cap

    def body(i, carry):
        for r in range(ROW_UNROLL):
            s = i * ROW_UNROLL + r
            rows_ref[pl.ds(s, 1), :] = h_ref[0, pl.ds(idx_ref[base + s], 1), :]
        return carry

    lax.fori_loop(0, cap // ROW_UNROLL, body, 0)
    o_ref[0] = rows_ref[...].astype(o_ref.dtype)


def _gather_rows(idx_flat, h, cap):
    b, t, d = h.shape
    return pl.pallas_call(
        functools.partial(_gather_body, cap=cap),
        grid_spec=pltpu.PrefetchScalarGridSpec(
            num_scalar_prefetch=1,
            grid=(b, N_EXPERTS),
            in_specs=[pl.BlockSpec((1, t, d), lambda i, e, idx: (i, 0, 0))],
            out_specs=pl.BlockSpec((1, cap, d), lambda i, e, idx: (e, i, 0)),
            scratch_shapes=[pltpu.VMEM((cap, d), F32)]),
        out_shape=jax.ShapeDtypeStruct((N_EXPERTS, b * cap, d), BF16),
        compiler_params=_params(("arbitrary", "arbitrary")),
        name="gather_rows",
    )(idx_flat, h)


def _combine_body(idx_ref, y_ref, o_ref, *, cap):
    e = pl.program_id(1)
    base = (pl.program_id(0) * pl.num_programs(1) + e) * cap

    @pl.when(e == 0)
    def _():
        o_ref[...] = jnp.zeros_like(o_ref)

    def body(i, carry):
        toks = [idx_ref[base + i * ROW_UNROLL + r] for r in range(ROW_UNROLL)]
        sums = [o_ref[0, pl.ds(toks[r], 1), :] + y_ref[0, pl.ds(i * ROW_UNROLL + r, 1), :] for r in range(ROW_UNROLL)]
        for r in range(ROW_UNROLL):
            o_ref[0, pl.ds(toks[r], 1), :] = sums[r]
        return carry

    lax.fori_loop(0, cap // ROW_UNROLL, body, 0)


def _combine_rows(idx_flat, y, b, t, cap):
    d = y.shape[-1]
    return pl.pallas_call(
        functools.partial(_combine_body, cap=cap),
        grid_spec=pltpu.PrefetchScalarGridSpec(
            num_scalar_prefetch=1,
            grid=(b, N_EXPERTS),
            in_specs=[pl.BlockSpec((1, cap, d), lambda i, e, idx: (e, i, 0))],
            out_specs=pl.BlockSpec((1, t, d), lambda i, e, idx: (i, 0, 0))),
        out_shape=jax.ShapeDtypeStruct((b, t, d), F32),
        compiler_params=_params(("arbitrary", "arbitrary")),
        name="combine_rows",
    )(idx_flat, y)


def _postnorm_body(x_ref, y_ref, g_ref, lng_ref, lnb_ref, o_ref):
    o_ref[0] = _layer_norm(DEEPNORM_ALPHA * x_ref[0] + g_ref[0] * y_ref[0]) * lng_ref[...] + lnb_ref[...]


def _post_norm(x, y, g, lng, lnb):
    b, t, d = x.shape
    tm = min(t, 1024)
    row = lambda i, j: (i, j, 0)
    return pl.pallas_call(
        _postnorm_body,
        grid=(b, t // tm),
        in_specs=[pl.BlockSpec((1, tm, d), row), pl.BlockSpec((1, tm, d), row),
                  pl.BlockSpec((1, 1, d), lambda i, j: (i, 0, 0)),
                  pl.BlockSpec((1, d), lambda i, j: (0, 0)), pl.BlockSpec((1, d), lambda i, j: (0, 0))],
        out_specs=pl.BlockSpec((1, tm, d), row),
        out_shape=jax.ShapeDtypeStruct((b, t, d), F32),
        compiler_params=_params(("arbitrary", "arbitrary")),
        name="post_norm",
    )(x, y, g, lng, lnb)


def _rope_tables(t):
    tok = jnp.arange(t, dtype=jnp.int32)
    lane = np.arange(LANES)
    d = lane % HEAD_DIM
    inv_freq = ROPE_BASE ** (-jnp.arange(0, HEAD_DIM // 2, 2, dtype=F32) / (HEAD_DIM // 2))
    freq = inv_freq[d % 16]
    use_col = jnp.asarray((d // 32) == 1)
    position = jnp.where(use_col[None, :], (tok % GRID_W)[:, None], (tok // GRID_W)[:, None])
    ang = position * freq[None, :]
    sign = jnp.asarray(np.where((d % 32) < 16, -1.0, 1.0).astype(np.float32))
    return jnp.cos(ang).astype(F32), (jnp.sin(ang) * sign[None, :]).astype(F32)


def _block_ones(width):
    idx = np.arange(width) // HEAD_DIM
    return jnp.asarray((idx[:, None] == idx[None, :]).astype(np.float32), dtype=BF16)


MIN_ROUTE_TILES = 16


def _moe(h2, aff, w_gate, w_up, w_down, layer):
    b, t, _ = h2.shape
    cap = EC_CAPACITY_FACTOR * t // N_EXPERTS
    nt = t // LANES
    aff_tiles = aff.reshape(b, N_EXPERTS, nt, LANES)
    if nt < MIN_ROUTE_TILES:
        aff_tiles = jnp.pad(aff_tiles, ((0, 0), (0, 0), (0, MIN_ROUTE_TILES - nt), (0, 0)))
    idx, gate = _route(aff_tiles, cap)
    idx_flat = idx.reshape(-1)
    xg = _gather_rows(idx_flat, h2, cap)
    gate_e = jnp.swapaxes(gate.reshape(b, N_EXPERTS, cap), 0, 1).reshape(N_EXPERTS, b * cap, 1)
    y = _expert_ffn(xg, w_gate, w_up, w_down, gate_e, layer)
    return _combine_rows(idx_flat, y, b, t, cap)


def kernel(x, c, ctx, c_ctx, w_mod, b_mod, w_in, q_norm_a, k_norm_a, w_fourier, b_fourier, sink_c, w_out, ln1_g,
           ln1_b, w_router, w_gate, w_up, w_down, ln2_g, ln2_b):
    b, t, d = x.shape
    cos, sin = _rope_tables(t)
    gq, gk = _block_ones(A_Q), _block_ones(A_KV)
    cc = jnp.zeros((MOD_ROWS, d), F32).at[:b].set(c).at[b].set(c_ctx)
    mod = _modulation(cc, w_mod, b_mod)
    w_in_bf = w_in.astype(BF16)
    w_out_bf = w_out.astype(BF16)

    x_lat, x_ctx = x, ctx
    for layer in range(DEPTH):
        update_ctx = layer < DEPTH - 1
        lat = [mod[layer, :b, i * d:(i + 1) * d][:, None, :] for i in range(6)]
        cm = [jnp.broadcast_to(mod[layer, b, i * d:(i + 1) * d][None, None, :], (b, 1, d)) for i in range(6)]
        qn = jnp.tile(q_norm_a[layer], A_HEADS)[None, :]
        kn = jnp.tile(k_norm_a[layer], A_KV_HEADS)[None, :]
        w_four = jax.scipy.linalg.block_diag(*[w_fourier[layer, g] for g in range(B_GROUPS)])
        bias = b_fourier[layer].reshape(1, B_W)
        lng1, lnb1 = ln1_g[layer][None, :], ln1_b[layer][None, :]
        lng2, lnb2 = ln2_g[layer][None, :], ln2_b[layer][None, :]
        wr_t = w_router[layer].T
        sink = sink_c[layer]

        proj_ctx = _in_projection(x_ctx, cm[0], cm[1], w_in_bf[layer], qn, kn, gq, gk, cos, sin, rope=False)
        qa_c, qc_c, u_c, ka_c, va_c, kc_c, vc_c = proj_ctx
        qa, qc, u, ka, va, kc, vc = _in_projection(x_lat, lat[0], lat[1], w_in_bf[layer], qn, kn, gq, gk, cos, sin,
                                                   rope=True)
        out_a = _attention(qa, ka_c, va_c, ka, va, heads_per_kv=4, mode="full")
        out_b = _fourier_mix(u, w_four, bias)
        out_c = _attention(qc, kc_c, vc_c, kc, vc, sink, heads_per_kv=2, mode="window")
        x1, h2, aff = _out_projection(out_a, out_b, out_c, w_out_bf[layer], x_lat, lat[2], lat[3], lat[4],
                                      lng1, lnb1, wr_t)
        y = _moe(h2, aff, w_gate, w_up, w_down, layer)
        x_lat = _post_norm(x1, y, lat[5], lng2, lnb2)

        if update_ctx:
            out_a_c = _attention(qa_c, ka_c, va_c, heads_per_kv=4, mode="none")
            out_b_c = _fourier_mix(u_c, w_four, bias)
            out_c_c = _attention(qc_c, kc_c, vc_c, sink=sink, heads_per_kv=2, mode="none")
            x1c, h2c, affc = _out_projection(out_a_c, out_b_c, out_c_c, w_out_bf[layer], x_ctx, cm[2], cm[3], cm[4],
                                             lng1, lnb1, wr_t)
            yc = _moe(h2c, affc, w_gate, w_up, w_down, layer)
            x_ctx = _post_norm(x1c, yc, cm[5], lng2, lnb2)
    return x_lat
```

```python
import functools

import numpy as np
import jax
import jax.numpy as jnp
from jax import lax
from jax.experimental import pallas as pl
from jax.experimental.pallas import tpu as pltpu

F32 = jnp.float32
BF16 = jnp.bfloat16

D_MODEL = 1024
DEPTH = 2
GRID_W = 64
HEAD_DIM = 64
ROPE_BASE = 10000.0
A_HEADS, A_KV_HEADS = 8, 2
B_GROUPS, B_GROUP_DIM = 4, 64
C_HEADS, C_KV_HEADS = 4, 2
WINDOW = 128
N_EXPERTS = 16
EC_CAPACITY_FACTOR = 2
EXPERT_FF = 2 * D_MODEL
A_Q, A_KV, B_W, C_Q, C_KV = 512, 128, 256, 256, 128
QU_WIDTH = A_Q + C_Q + B_W
IN_WIDTH = QU_WIDTH + 2 * A_KV + 2 * C_KV
LN_EPS = 1e-5
RMS_EPS = 1e-6
NEG_INF = -1e30
LOG2E = 1.4426950408889634
SAFE_LOGIT = 60.0
DEEPNORM_ALPHA = (2 * DEPTH) ** 0.25
MOD_ROWS = 16
LANES = 128
ROW_SUBTILES = 2
WINDOW_Q_ROWS = 256
VMEM_LIMIT = 56 * 1024 * 1024

HIGHEST = lax.Precision.HIGHEST
NT_DIMS = (((1,), (1,)), ((), ()))


def _params(sem):
    return pltpu.CompilerParams(dimension_semantics=sem, vmem_limit_bytes=VMEM_LIMIT)


def _layer_norm(x):
    mu = jnp.mean(x, axis=-1, keepdims=True)
    xc = x - mu
    var = jnp.mean(xc * xc, axis=-1, keepdims=True)
    return xc * lax.rsqrt(var + LN_EPS)


def _silu(x):
    return x * (1.0 / (1.0 + jnp.exp(-x)))


def _mod_body(c_ref, w_ref, b_ref, o_ref):
    s = _silu(c_ref[...])
    o_ref[0] = jnp.dot(s, w_ref[0], precision=HIGHEST, preferred_element_type=F32) + b_ref[0]


def _modulation(cc, w_mod, b_mod):
    tn = 1024
    n = w_mod.shape[-1]
    return pl.pallas_call(
        _mod_body,
        grid=(DEPTH, n // tn),
        in_specs=[
            pl.BlockSpec((MOD_ROWS, D_MODEL), lambda l, j: (0, 0)),
            pl.BlockSpec((1, D_MODEL, tn), lambda l, j: (l, 0, j)),
            pl.BlockSpec((1, 1, tn), lambda l, j: (l, 0, j)),
        ],
        out_specs=pl.BlockSpec((1, MOD_ROWS, tn), lambda l, j: (l, 0, j)),
        out_shape=jax.ShapeDtypeStruct((DEPTH, MOD_ROWS, n), F32),
        compiler_params=_params(("arbitrary", "arbitrary")),
        name="modulation",
    )(cc, w_mod, b_mod.reshape(DEPTH, 1, n))


def _rope(x, cos, sin_signed, first_half):
    outs = []
    for j in range(x.shape[1] // LANES):
        xj = x[:, LANES * j:LANES * (j + 1)]
        partner = jnp.where(first_half, pltpu.roll(xj, LANES - 16, 1), pltpu.roll(xj, 16, 1))
        outs.append(xj * cos + partner * sin_signed)
    return jnp.concatenate(outs, axis=1) if len(outs) > 1 else outs[0]


def _dup_kv_heads(k, copies):
    lane = lax.broadcasted_iota(jnp.int32, (1, LANES), 1)
    lo = lane < HEAD_DIM
    r = pltpu.roll(k, HEAD_DIM, 1)
    d0 = jnp.where(lo, k, r)
    d1 = jnp.where(lo, r, k)
    reps = copies // 2
    return jnp.concatenate([d0] * reps + [d1] * reps, axis=1)


def _inproj_body(x_ref, sh_ref, sc_ref, w_ref, qn_ref, kn_ref, gq_ref, gk_ref, cos_ref, sin_ref,
                 qa_ref, qc_ref, u_ref, ka_ref, va_ref, kc_ref, vc_ref, *, rope):
    tm = x_ref.shape[1]
    sub = tm // ROW_SUBTILES
    out_refs = (qa_ref, qc_ref, u_ref, ka_ref, va_ref, kc_ref, vc_ref)
    for i in range(ROW_SUBTILES):
        rows = pl.ds(i * sub, sub)
        tables = (cos_ref[rows, :], sin_ref[rows, :]) if rope else None
        outs = _inproj_rows(x_ref[0, rows, :], sh_ref[0], sc_ref[0], w_ref, qn_ref[...], kn_ref[...], gq_ref, gk_ref,
                            tables)
        for ref, val in zip(out_refs, outs):
            ref[0, rows, :] = val


def _inproj_rows(x, sh, sc, w_ref, qn, kn, gq_ref, gk_ref, tables):
    h = _layer_norm(x) * (1.0 + sc) + sh
    p = jnp.dot(h.astype(BF16), w_ref[...], preferred_element_type=F32)
    qa = p[:, :A_Q]
    qc = p[:, A_Q:A_Q + C_Q]
    u = p[:, A_Q + C_Q:QU_WIDTH]
    ka = p[:, QU_WIDTH:QU_WIDTH + A_KV]
    va = p[:, QU_WIDTH + A_KV:QU_WIDTH + 2 * A_KV]
    kc = p[:, QU_WIDTH + 2 * A_KV:QU_WIDTH + 2 * A_KV + C_KV]
    vc = p[:, QU_WIDTH + 2 * A_KV + C_KV:]
    msq = jnp.dot((qa * qa).astype(BF16), gq_ref[...], preferred_element_type=F32) * (1.0 / HEAD_DIM)
    qa = qa * lax.rsqrt(msq + RMS_EPS) * qn
    msk = jnp.dot((ka * ka).astype(BF16), gk_ref[...], preferred_element_type=F32) * (1.0 / HEAD_DIM)
    ka = ka * lax.rsqrt(msk + RMS_EPS) * kn
    if tables is not None:
        cos, sin = tables
        lane = lax.broadcasted_iota(jnp.int32, (1, LANES), 1)
        first_half = (lane % 32) < 16
        qa = _rope(qa, cos, sin, first_half)
        qc = _rope(qc, cos, sin, first_half)
        ka = _rope(ka, cos, sin, first_half)
        kc = _rope(kc, cos, sin, first_half)
    scale = HEAD_DIM ** -0.5 * LOG2E
    return ((qa * scale).astype(BF16), (qc * scale).astype(BF16), u,
            _dup_kv_heads(ka, A_HEADS // A_KV_HEADS).astype(BF16),
            _dup_kv_heads(va, A_HEADS // A_KV_HEADS).astype(BF16),
            _dup_kv_heads(kc, C_HEADS // C_KV_HEADS).astype(BF16),
            _dup_kv_heads(vc, C_HEADS // C_KV_HEADS).astype(BF16))


def _in_projection(x, sh, sc, w_bf16, qn, kn, gq, gk, cos, sin, *, rope):
    b, t, d = x.shape
    tm = min(t, 512)
    row = lambda i, j: (i, j, 0)
    per_b = lambda i, j: (i, 0, 0)
    const = lambda i, j: (0, 0)
    tab = (lambda i, j: (j, 0)) if rope else const
    outs = [(A_Q, BF16), (C_Q, BF16), (B_W, F32), (4 * A_KV, BF16), (4 * A_KV, BF16),
            (2 * C_KV, BF16), (2 * C_KV, BF16)]
    return pl.pallas_call(
        functools.partial(_inproj_body, rope=rope),
        grid=(b, t // tm),
        in_specs=[
            pl.BlockSpec((1, tm, d), row),
            pl.BlockSpec((1, 1, d), per_b),
            pl.BlockSpec((1, 1, d), per_b),
            pl.BlockSpec((d, IN_WIDTH), const),
            pl.BlockSpec((1, A_Q), const),
            pl.BlockSpec((1, A_KV), const),
            pl.BlockSpec((A_Q, A_Q), const),
            pl.BlockSpec((A_KV, A_KV), const),
            pl.BlockSpec((tm, LANES), tab),
            pl.BlockSpec((tm, LANES), tab),
        ],
        out_specs=[pl.BlockSpec((1, tm, w), row) for w, _ in outs],
        out_shape=[jax.ShapeDtypeStruct((b, t, w), dt) for w, dt in outs],
        compiler_params=_params(("arbitrary", "arbitrary")),
        name="in_projection",
    )(x, sh, sc, w_bf16, qn, kn, gq, gk, cos, sin)


def _attn_body(*refs, heads_per_kv, mode, has_sink, tq, t_lat, kc_full):
    refs = list(refs)
    q_ref, kctx_ref, vctx_ref = refs[:3]
    pos = 3
    klat_ref = vlat_ref = sink_ref = None
    if mode != "none":
        klat_ref, vlat_ref = refs[pos:pos + 2]
        pos += 2
    if has_sink:
        sink_ref = refs[pos]
        pos += 1
    o_ref, kmax_ref = refs[pos:pos + 2]

    width = heads_per_kv * HEAD_DIM
    kvh = pl.program_id(1)
    qi = pl.program_id(2)
    q = q_ref[0]
    lane = lax.broadcasted_iota(jnp.int32, (1, width), 1)
    head_lanes = [(lane // HEAD_DIM) == g for g in range(heads_per_kv)]
    sinks = [sink_ref[kvh * heads_per_kv + g] * LOG2E for g in range(heads_per_kv)] if has_sink else None
    sub = min(tq, WINDOW_Q_ROWS) if mode == "window" else tq

    def per_head_lanes(cols):
        full = jnp.broadcast_to(cols[0], (sub, width))
        for g in range(1, heads_per_kv):
            full = jnp.where(head_lanes[g], cols[g], full)
        return full

    def key_sq_norm_max(ref):
        rows = ref.shape[1]
        step_rows = min(rows, 1024)

        def body(c, best):
            kf = ref[0, pl.ds(pl.multiple_of(c * step_rows, step_rows), step_rows), :].astype(F32)
            return jnp.maximum(best, jnp.max(jnp.sum(kf * kf, axis=1, keepdims=True), axis=0, keepdims=True))
        best = lax.fori_loop(0, rows // step_rows, body, jnp.zeros((1, 1), F32))
        return best[0, 0] * (1.0 / heads_per_kv)

    @pl.when(qi == 0)
    def _():
        best = key_sq_norm_max(kctx_ref)
        if mode != "none":
            best = jnp.maximum(best, key_sq_norm_max(klat_ref))
        kmax_ref[0] = best

    def sweep(stabilised):
        groups = [sweep_rows(stabilised, r0) for r0 in range(0, tq, sub)]
        return jnp.concatenate(groups, axis=0) if len(groups) > 1 else groups[0]

    def sweep_rows(stabilised, r0):
        q_rows = q[r0:r0 + sub]
        q_stack = jnp.concatenate([jnp.where(hl, q_rows, jnp.zeros_like(q_rows)) for hl in head_lanes], axis=0)

        def by_head(stacked):
            return [stacked[g * sub:(g + 1) * sub] for g in range(heads_per_kv)]

        def step(carry, k, v, mask=None):
            m, l, acc = carry
            s = lax.dot_general(q_stack, k, NT_DIMS, preferred_element_type=F32)
            if mask is not None:
                s = jnp.where(jnp.concatenate([mask] * heads_per_kv, axis=0), s, NEG_INF)
            if stabilised:
                m_new = jnp.maximum(m, jnp.max(s, axis=1, keepdims=True))
                alpha = jnp.exp2(m - m_new)
                p = jnp.exp2(s - m_new)
                l = alpha * l + jnp.sum(p, axis=1, keepdims=True)
                acc = acc * per_head_lanes(by_head(alpha))
            else:
                m_new = m
                p = jnp.exp2(s)
                l = l + jnp.sum(p, axis=1, keepdims=True)
            pv = jnp.dot(p.astype(BF16), v, preferred_element_type=F32)
            for g, part in enumerate(by_head(pv)):
                acc = acc + jnp.where(head_lanes[g], part, 0.0)
            return m_new, l, acc

        rows_of = lambda vals: jnp.concatenate([jnp.full((sub, 1), v, F32) for v in vals], axis=0)
        if not has_sink:
            m0 = jnp.full((heads_per_kv * sub, 1), NEG_INF if stabilised else 0.0, F32)
            l0 = jnp.zeros((heads_per_kv * sub, 1), F32)
        elif stabilised:
            m0 = rows_of(sinks)
            l0 = jnp.ones((heads_per_kv * sub, 1), F32)
        else:
            m0 = jnp.zeros((heads_per_kv * sub, 1), F32)
            l0 = jnp.exp2(rows_of(sinks))
        carry = (m0, l0, jnp.zeros((sub, width), F32))
        carry = step(carry, kctx_ref[0], vctx_ref[0])
        if mode == "full":
            for c in range(t_lat // kc_full):
                rows = pl.ds(c * kc_full, kc_full)
                carry = step(carry, klat_ref[0, rows, :], vlat_ref[0, rows, :])
        elif mode == "window":
            span = sub + 2 * WINDOW
            q0 = qi * tq + r0
            k0 = pl.multiple_of(jnp.clip(q0 - WINDOW, 0, t_lat - span), WINDOW)
            qpos = q0 + lax.broadcasted_iota(jnp.int32, (sub, 1), 0)
            kpos = k0 + lax.broadcasted_iota(jnp.int32, (1, span), 1)
            in_window = jnp.abs(kpos - qpos) <= WINDOW
            carry = step(carry, klat_ref[0, pl.ds(k0, span), :], vlat_ref[0, pl.ds(k0, span), :], in_window)
        _, l, acc = carry
        return acc * per_head_lanes(by_head(1.0 / l))

    qf = q.astype(F32)
    qf = qf * qf
    head_sq = [jnp.sum(jnp.where(hl, qf, 0.0), axis=1, keepdims=True) for hl in head_lanes]
    q_sq = jnp.max(functools.reduce(jnp.maximum, head_sq), axis=0, keepdims=True)[0, 0]
    bounded = q_sq * kmax_ref[0] <= SAFE_LOGIT * SAFE_LOGIT
    if has_sink:
        for g in range(heads_per_kv):
            bounded = jnp.logical_and(bounded, jnp.abs(sinks[g]) <= SAFE_LOGIT)
    out = lax.cond(bounded, lambda: sweep(False), lambda: sweep(True))
    o_ref[0] = out.astype(o_ref.dtype)


def _attention(q, kctx, vctx, klat=None, vlat=None, sink=None, *, heads_per_kv, mode):
    b, t, _ = q.shape
    n_kv = 2
    width = heads_per_kv * HEAD_DIM
    t_ctx = kctx.shape[1]
    tq = min(t, 512)
    qmap = lambda i, h, j: (i, j, h)
    kmap = lambda i, h, j: (i, 0, h)
    in_specs = [pl.BlockSpec((1, tq, width), qmap),
                pl.BlockSpec((1, t_ctx, width), kmap),
                pl.BlockSpec((1, t_ctx, width), kmap)]
    args = [q, kctx, vctx]
    t_lat = 0
    if mode != "none":
        t_lat = klat.shape[1]
        in_specs += [pl.BlockSpec((1, t_lat, width), kmap), pl.BlockSpec((1, t_lat, width), kmap)]
        args += [klat, vlat]
    if sink is not None:
        in_specs.append(pl.BlockSpec(memory_space=pltpu.SMEM))
        args.append(sink)
    body = functools.partial(_attn_body, heads_per_kv=heads_per_kv, mode=mode, has_sink=sink is not None,
                             tq=tq, t_lat=t_lat, kc_full=min(max(t_lat, 1), 1024))
    return pl.pallas_call(
        body,
        grid=(b, n_kv, t // tq),
        in_specs=in_specs,
        out_specs=pl.BlockSpec((1, tq, width), qmap),
        out_shape=jax.ShapeDtypeStruct((b, t, n_kv * width), BF16),
        scratch_shapes=[pltpu.SMEM((1,), F32)],
        compiler_params=_params(("arbitrary", "arbitrary", "arbitrary")),
        name=f"attention_{mode}_g{heads_per_kv}",
    )(*args)


def _split_bf16(x):
    hi = x.astype(BF16)
    return hi, (x - hi.astype(F32)).astype(BF16)


def _dot3(a, b):
    a_hi, a_lo = _split_bf16(a)
    b_hi, b_lo = _split_bf16(b)
    d = lambda x, y: jnp.dot(x, y, preferred_element_type=F32)
    return d(a_hi, b_hi) + (d(a_hi, b_lo) + d(a_lo, b_hi))


def _four_chan_body(u_ref, wblk_ref, cc_ref, sc_ref, z_ref):
    w = wblk_ref[...]
    wc = jnp.dot(cc_ref[...], w, precision=HIGHEST, preferred_element_type=F32)
    ws = jnp.dot(sc_ref[...], w, precision=HIGHEST, preferred_element_type=F32)
    u = u_ref[0]
    z_ref[0, 0] = _dot3(u, wc)
    z_ref[0, 1] = -_dot3(u, ws)


def _four_rows_body(z_ref, m_ref, tc_ref, ts_ref, a_ref):
    n1 = z_ref.shape[2]
    z = jnp.concatenate([z_ref[0, 0], z_ref[0, 1]], axis=0)
    a = _dot3(m_ref[...], z)
    ar, ai = a[:n1], a[n1:]
    tc, ts = tc_ref[...], ts_ref[...]
    a_ref[0, 0] = ar * tc + ai * ts
    a_ref[0, 1] = ai * tc - ar * ts


def _four_cols_body(a_ref, cs_ref, bias_ref, y_ref):
    kb = a_ref.shape[2]
    for j in range(kb):
        rhs = jnp.concatenate([a_ref[0, 0, j], a_ref[0, 1, j]], axis=0)
        y_ref[0, :, j, :] = _dot3(cs_ref[...], rhs) + bias_ref[...]


def _dft_constants(t):
    n = int(round(t ** 0.5))
    assert n * n == t
    k = np.arange(n)
    ang = 2.0 * np.pi * np.outer(k, k) / n
    c, s = np.cos(ang), np.sin(ang)
    rows = np.block([[c, s], [-s, c]]).astype(np.float32)
    cols = np.concatenate([c, s], axis=1).astype(np.float32)
    tw = 2.0 * np.pi * np.outer(k, k) / t
    tc = np.repeat(np.cos(tw), B_W, axis=1).astype(np.float32)
    ts = np.repeat(np.sin(tw), B_W, axis=1).astype(np.float32)
    return n, rows, cols, tc, ts


def _channel_constants(t):
    k = np.arange(B_GROUP_DIM)
    ang = 2.0 * np.pi * np.outer(k, k) / B_GROUP_DIM
    norm = 1.0 / np.sqrt(float(t) * B_GROUP_DIM)
    eye = np.eye(B_GROUPS)
    cc = np.kron(eye, np.cos(ang) * norm).astype(np.float32)
    sc = np.kron(eye, np.sin(ang) * norm).astype(np.float32)
    return cc, sc


def _fourier_mix(u, w_blockdiag, bias):
    b, t, _ = u.shape
    n, rows, cols, tc, ts = _dft_constants(t)
    cc, sc = _channel_constants(t)
    tm = min(t, 1024)
    z = pl.pallas_call(
        _four_chan_body,
        grid=(b, t // tm),
        in_specs=[pl.BlockSpec((1, tm, B_W), lambda i, j: (i, j, 0)),
                  pl.BlockSpec((B_W, B_W), lambda i, j: (0, 0)),
                  pl.BlockSpec((B_W, B_W), lambda i, j: (0, 0)),
                  pl.BlockSpec((B_W, B_W), lambda i, j: (0, 0))],
        out_specs=pl.BlockSpec((1, 2, tm, B_W), lambda i, j: (i, 0, j, 0)),
        out_shape=jax.ShapeDtypeStruct((b, 2, t, B_W), F32),
        compiler_params=_params(("arbitrary", "arbitrary")),
        name="fourier_channels",
    )(u, w_blockdiag, jnp.asarray(cc), jnp.asarray(sc))
    cols_total = n * B_W
    tc_cols = min(cols_total, 4096)
    a = pl.pallas_call(
        _four_rows_body,
        grid=(b, cols_total // tc_cols),
        in_specs=[pl.BlockSpec((1, 2, n, tc_cols), lambda i, j: (i, 0, 0, j)),
                  pl.BlockSpec((2 * n, 2 * n), lambda i, j: (0, 0)),
                  pl.BlockSpec((n, tc_cols), lambda i, j: (0, j)),
                  pl.BlockSpec((n, tc_cols), lambda i, j: (0, j))],
        out_specs=pl.BlockSpec((1, 2, n, tc_cols), lambda i, j: (i, 0, 0, j)),
        out_shape=jax.ShapeDtypeStruct((b, 2, n, cols_total), F32),
        compiler_params=_params(("arbitrary", "arbitrary")),
        name="fourier_rows",
    )(z.reshape(b, 2, n, cols_total), jnp.asarray(rows), jnp.asarray(tc), jnp.asarray(ts))
    kb = 8
    y = pl.pallas_call(
        _four_cols_body,
        grid=(b, n // kb),
        in_specs=[pl.BlockSpec((1, 2, kb, n, B_W), lambda i, j: (i, 0, j, 0, 0)),
                  pl.BlockSpec((n, 2 * n), lambda i, j: (0, 0)),
                  pl.BlockSpec((1, B_W), lambda i, j: (0, 0))],
        out_specs=pl.BlockSpec((1, n, kb, B_W), lambda i, j: (i, 0, j, 0)),
        out_shape=jax.ShapeDtypeStruct((b, n, n, B_W), F32),
        compiler_params=_params(("arbitrary", "arbitrary")),
        name="fourier_cols",
    )(a.reshape(b, 2, n, n, B_W), jnp.asarray(cols), bias)
    return y.reshape(b, t, B_W)


def _outproj_body(a_ref, f_ref, c_ref, w_ref, x_ref, g1_ref, sh2_ref, sc2_ref, lng_ref, lnb_ref, wr_ref,
                  x1_ref, h2_ref, aff_ref):
    tm = x_ref.shape[1]
    sub = tm // ROW_SUBTILES
    for i in range(ROW_SUBTILES):
        rows = pl.ds(i * sub, sub)
        o = jnp.dot(a_ref[0, rows, :], w_ref[:A_Q], preferred_element_type=F32)
        o = o + jnp.dot(f_ref[0, rows, :].astype(BF16), w_ref[A_Q:A_Q + B_W], preferred_element_type=F32)
        o = o + jnp.dot(c_ref[0, rows, :], w_ref[A_Q + B_W:], preferred_element_type=F32)
        x1 = _layer_norm(DEEPNORM_ALPHA * x_ref[0, rows, :] + g1_ref[0] * o) * lng_ref[...] + lnb_ref[...]
        x1_ref[0, rows, :] = x1
        h2 = _layer_norm(x1) * (1.0 + sc2_ref[0]) + sh2_ref[0]
        h2_ref[0, rows, :] = h2
        logits = lax.dot_general(wr_ref[...], h2, NT_DIMS, precision=HIGHEST, preferred_element_type=F32)
        e = jnp.exp(logits - jnp.max(logits, axis=0, keepdims=True))
        aff_ref[0, :, rows] = e / jnp.sum(e, axis=0, keepdims=True)


def _out_projection(oa, of, oc, w_bf16, x, g1, sh2, sc2, lng, lnb, wr_t):
    b, t, d = x.shape
    tm = min(t, 512)
    row = lambda i, j: (i, j, 0)
    per_b = lambda i, j: (i, 0, 0)
    const = lambda i, j: (0, 0)
    return pl.pallas_call(
        _outproj_body,
        grid=(b, t // tm),
        in_specs=[pl.BlockSpec((1, tm, A_Q), row),
                  pl.BlockSpec((1, tm, B_W), row),
                  pl.BlockSpec((1, tm, C_Q), row),
                  pl.BlockSpec((d, d), const),
                  pl.BlockSpec((1, tm, d), row),
                  pl.BlockSpec((1, 1, d), per_b),
                  pl.BlockSpec((1, 1, d), per_b),
                  pl.BlockSpec((1, 1, d), per_b),
                  pl.BlockSpec((1, d), const),
                  pl.BlockSpec((1, d), const),
                  pl.BlockSpec((N_EXPERTS, d), const)],
        out_specs=[pl.BlockSpec((1, tm, d), row),
                   pl.BlockSpec((1, tm, d), row),
                   pl.BlockSpec((1, N_EXPERTS, tm), lambda i, j: (i, 0, j))],
        out_shape=[jax.ShapeDtypeStruct((b, t, d), F32),
                   jax.ShapeDtypeStruct((b, t, d), F32),
                   jax.ShapeDtypeStruct((b, N_EXPERTS, t), F32)],
        compiler_params=_params(("arbitrary", "arbitrary")),
        name="out_projection",
    )(oa, of, oc, w_bf16, x, g1, sh2, sc2, lng, lnb, wr_t)


def _ffn_body(x_ref, wg_ref, wu_ref, wd_ref, gate_ref, o_ref):
    @pl.when(pl.program_id(2) == 0)
    def _():
        o_ref[...] = jnp.zeros_like(o_ref)

    x = x_ref[0]
    a = jnp.dot(x, wg_ref[0, 0].astype(BF16), preferred_element_type=F32)
    up = jnp.dot(x, wu_ref[0, 0].astype(BF16), preferred_element_type=F32)
    h = (_silu(a) * up).astype(BF16)
    y = jnp.dot(h, wd_ref[0, 0].astype(BF16), preferred_element_type=F32)
    o_ref[0] += gate_ref[0] * y


def _expert_ffn(xg, w_gate, w_up, w_down, gate, layer):
    e, m, d = xg.shape
    ff = w_gate.shape[-1]
    tm = min(m, 2048)
    tf = 512
    return pl.pallas_call(
        _ffn_body,
        grid=(e, m // tm, ff // tf),
        in_specs=[pl.BlockSpec((1, tm, d), lambda i, j, k: (i, j, 0)),
                  pl.BlockSpec((1, 1, d, tf), lambda i, j, k: (layer, i, 0, k)),
                  pl.BlockSpec((1, 1, d, tf), lambda i, j, k: (layer, i, 0, k)),
                  pl.BlockSpec((1, 1, tf, d), lambda i, j, k: (layer, i, k, 0)),
                  pl.BlockSpec((1, tm, 1), lambda i, j, k: (i, j, 0))],
        out_specs=pl.BlockSpec((1, tm, d), lambda i, j, k: (i, j, 0)),
        out_shape=jax.ShapeDtypeStruct((e, m, d), F32),
        compiler_params=_params(("arbitrary", "arbitrary", "arbitrary")),
        name="expert_ffn",
    )(xg, w_gate, w_up, w_down, gate)


def _route_body(aff_ref, tri_ref, lower_ref, idx_ref, gate_ref, cum_ref, *, cap):
    nb, n_exp, nt, _ = aff_ref.shape
    lists = nb * n_exp
    rows = n_exp * nt
    a = aff_ref[...].reshape(lists, nt, LANES)

    def count(mask):
        ones = jnp.where(mask, 1.0, 0.0)
        return jnp.sum(jnp.sum(ones, axis=1, keepdims=True), axis=2, keepdims=True)

    def token_prefix(mask):
        flat = jnp.where(mask, 1.0, 0.0).astype(BF16).reshape(nb * rows, LANES)
        in_row = jnp.dot(flat, tri_ref[...], preferred_element_type=F32)
        row_total = jnp.broadcast_to(in_row[:, LANES - 1:LANES], in_row.shape).astype(BF16)
        before = [jnp.dot(lower_ref[...], row_total[i * rows:(i + 1) * rows], preferred_element_type=F32)
                  for i in range(nb)]
        return (in_row + jnp.concatenate(before, axis=0)).reshape(lists, nt, LANES)

    def enough(cand):
        return count(a >= cand) >= cap

    power = jnp.ones((lists, 1, 1), F32)
    for k in range(6, -1, -1):
        too_few = jnp.logical_not(enough(power * 2.0 ** -(2 ** k - 1)))
        power = jnp.where(too_few, power * 2.0 ** -(2 ** k), power)
    base = jnp.where(enough(power), power, 0.0)

    def refine(_, carry):
        thr, digit = carry
        digit = digit * 0.5
        cand = thr + digit
        return jnp.where(enough(cand), cand, thr), digit

    thr, _ = lax.fori_loop(0, 23, refine, (base, base))
    above = a > thr
    tied = a == thr
    need = cap - count(above)
    chosen = above | (tied & (token_prefix(tied) <= need))
    cum_ref[...] = token_prefix(chosen)

    slot = lax.broadcasted_iota(jnp.int32, (cap, 1), 0).astype(F32)
    tile_eye = lax.broadcasted_iota(jnp.int32, (nt, nt), 0) == lax.broadcasted_iota(jnp.int32, (nt, nt), 1)
    tile_id = lax.broadcasted_iota(jnp.int32, (cap, nt), 1).astype(F32)
    lane_id = lax.broadcasted_iota(jnp.int32, (cap, LANES), 1).astype(F32)

    blk = min(cap, LANES)
    blk_eye = lax.broadcasted_iota(jnp.int32, (blk, blk), 0) == lax.broadcasted_iota(jnp.int32, (blk, blk), 1)

    def as_row(col):
        parts = [jnp.sum(jnp.where(blk_eye, col[j * blk:(j + 1) * blk], 0.0), axis=0, keepdims=True)
                 for j in range(cap // blk)]
        return jnp.concatenate(parts, axis=1) if len(parts) > 1 else parts[0]

    def per_list(i, carry):
        cum = cum_ref[i]
        aff = aff_ref[i // n_exp, i % n_exp]
        tile_end = jnp.sum(jnp.where(tile_eye, cum[:, LANES - 1:LANES], 0.0), axis=0, keepdims=True)
        tile_of_slot = jnp.sum(jnp.where(tile_end <= slot, 1.0, 0.0), axis=1, keepdims=True)
        pick_tile = jnp.where(tile_id == tile_of_slot, 1.0, 0.0).astype(BF16)
        cum_hi = jnp.floor(cum * (1.0 / 16.0))
        g1 = aff.astype(BF16)
        r1 = aff - g1.astype(F32)
        g2 = r1.astype(BF16)
        g3 = (r1 - g2.astype(F32)).astype(BF16)
        table = jnp.concatenate([cum_hi.astype(BF16), (cum - 16.0 * cum_hi).astype(BF16), g1, g2, g3], axis=1)
        got = jnp.dot(pick_tile, table, preferred_element_type=F32)
        counts = 16.0 * got[:, :LANES] + got[:, LANES:2 * LANES]
        within = jnp.sum(jnp.where(counts <= slot, 1.0, 0.0), axis=1, keepdims=True)
        affs = (got[:, 2 * LANES:3 * LANES] + got[:, 3 * LANES:4 * LANES]) + got[:, 4 * LANES:]
        gate = jnp.sum(jnp.where(lane_id == within, affs, 0.0), axis=1, keepdims=True)
        idx_ref[pl.ds(i, 1), :] = as_row(LANES * tile_of_slot + within).astype(jnp.int32)
        gate_ref[pl.ds(i, 1), :] = as_row(gate)
        return carry

    lax.fori_loop(0, lists, per_list, 0)


def _route(aff, cap):
    b, n_exp, nt, _ = aff.shape
    rows = n_exp * nt
    r = np.arange(rows)
    same_expert = (r[:, None] // nt) == (r[None, :] // nt)
    lane = np.arange(LANES)
    tri = jnp.asarray((lane[:, None] <= lane[None, :]).astype(np.float32), dtype=BF16)
    lower = jnp.asarray((same_expert & (r[None, :] < r[:, None])).astype(np.float32), dtype=BF16)
    whole = lambda i: (0, 0, 0, 0)
    return pl.pallas_call(
        functools.partial(_route_body, cap=cap),
        grid=(1,),
        in_specs=[pl.BlockSpec((b, n_exp, nt, LANES), whole),
                  pl.BlockSpec((LANES, LANES), lambda i: (0, 0)),
                  pl.BlockSpec((rows, rows), lambda i: (0, 0))],
        out_specs=[pl.BlockSpec((b * n_exp, cap), lambda i: (0, 0)),
                   pl.BlockSpec((b * n_exp, cap), lambda i: (0, 0))],
        out_shape=[jax.ShapeDtypeStruct((b * n_exp, cap), jnp.int32),
                   jax.ShapeDtypeStruct((b * n_exp, cap), F32)],
        scratch_shapes=[pltpu.VMEM((b * n_exp, nt, LANES), F32)],
        compiler_params=_params(("arbitrary",)),
        name="route",
    )(aff, tri, lower)


ROW_UNROLL = 8


def _gather_body(idx_ref, h_ref, o_ref, rows_ref, *, cap):
    base = (pl.program_id(0) * pl.num_programs(1) + pl.program_id(1)) * cap

    def body(i, carry):
        for r in range(ROW_UNROLL):
            s = i * ROW_UNROLL + r
            rows_ref[pl.ds(s, 1), :] = h_ref[0, pl.ds(idx_ref[base + s], 1), :]
        return carry

    lax.fori_loop(0, cap // ROW_UNROLL, body, 0)
    o_ref[0] = rows_ref[...].astype(o_ref.dtype)


def _gather_rows(idx_flat, h, cap):
    b, t, d = h.shape
    return pl.pallas_call(
        functools.partial(_gather_body, cap=cap),
        grid_spec=pltpu.PrefetchScalarGridSpec(
            num_scalar_prefetch=1,
            grid=(b, N_EXPERTS),
            in_specs=[pl.BlockSpec((1, t, d), lambda i, e, idx: (i, 0, 0))],
            out_specs=pl.BlockSpec((1, cap, d), lambda i, e, idx: (e, i, 0)),
            scratch_shapes=[pltpu.VMEM((cap, d), F32)]),
        out_shape=jax.ShapeDtypeStruct((N_EXPERTS, b * cap, d), BF16),
        compiler_params=_params(("arbitrary", "arbitrary")),
        name="gather_rows",
    )(idx_flat, h)


def _combine_body(idx_ref, y_ref, o_ref, *, cap):
    e = pl.program_id(1)
    base = (pl.program_id(0) * pl.num_programs(1) + e) * cap

    @pl.when(e == 0)
    def _():
        o_ref[...] = jnp.zeros_like(o_ref)

    def body(i, carry):
        toks = [idx_ref[base + i * ROW_UNROLL + r] for r in range(ROW_UNROLL)]
        sums = [o_ref[0, pl.ds(toks[r], 1), :] + y_ref[0, pl.ds(i * ROW_UNROLL + r, 1), :] for r in range(ROW_UNROLL)]
        for r in range(ROW_UNROLL):
            o_ref[0, pl.ds(toks[r], 1), :] = sums[r]
        return carry

    lax.fori_loop(0, cap // ROW_UNROLL, body, 0)


def _combine_rows(idx_flat, y, b, t, cap):
    d = y.shape[-1]
    return pl.pallas_call(
        functools.partial(_combine_body, cap=cap),
        grid_spec=pltpu.PrefetchScalarGridSpec(
            num_scalar_prefetch=1,
            grid=(b, N_EXPERTS),
            in_specs=[pl.BlockSpec((1, cap, d), lambda i, e, idx: (e, i, 0))],
            out_specs=pl.BlockSpec((1, t, d), lambda i, e, idx: (i, 0, 0))),
        out_shape=jax.ShapeDtypeStruct((b, t, d), F32),
        compiler_params=_params(("arbitrary", "arbitrary")),
        name="combine_rows",
    )(idx_flat, y)


def _postnorm_body(x_ref, y_ref, g_ref, lng_ref, lnb_ref, o_ref):
    o_ref[0] = _layer_norm(DEEPNORM_ALPHA * x_ref[0] + g_ref[0] * y_ref[0]) * lng_ref[...] + lnb_ref[...]


def _post_norm(x, y, g, lng, lnb):
    b, t, d = x.shape
    tm = min(t, 1024)
    row = lambda i, j: (i, j, 0)
    return pl.pallas_call(
        _postnorm_body,
        grid=(b, t // tm),
        in_specs=[pl.BlockSpec((1, tm, d), row), pl.BlockSpec((1, tm, d), row),
                  pl.BlockSpec((1, 1, d), lambda i, j: (i, 0, 0)),
                  pl.BlockSpec((1, d), lambda i, j: (0, 0)), pl.BlockSpec((1, d), lambda i, j: (0, 0))],
        out_specs=pl.BlockSpec((1, tm, d), row),
        out_shape=jax.ShapeDtypeStruct((b, t, d), F32),
        compiler_params=_params(("arbitrary", "arbitrary")),
        name="post_norm",
    )(x, y, g, lng, lnb)


def _rope_tables(t):
    tok = jnp.arange(t, dtype=jnp.int32)
    lane = np.arange(LANES)
    d = lane % HEAD_DIM
    inv_freq = ROPE_BASE ** (-jnp.arange(0, HEAD_DIM // 2, 2, dtype=F32) / (HEAD_DIM // 2))
    freq = inv_freq[d % 16]
    use_col = jnp.asarray((d // 32) == 1)
    position = jnp.where(use_col[None, :], (tok % GRID_W)[:, None], (tok // GRID_W)[:, None])
    ang = position * freq[None, :]
    sign = jnp.asarray(np.where((d % 32) < 16, -1.0, 1.0).astype(np.float32))
    return jnp.cos(ang).astype(F32), (jnp.sin(ang) * sign[None, :]).astype(F32)


def _block_ones(width):
    idx = np.arange(width) // HEAD_DIM
    return jnp.asarray((idx[:, None] == idx[None, :]).astype(np.float32), dtype=BF16)


MIN_ROUTE_TILES = 16


def _moe(h2, aff, w_gate, w_up, w_down, layer):
    b, t, _ = h2.shape
    cap = EC_CAPACITY_FACTOR * t // N_EXPERTS
    nt = t // LANES
    aff_tiles = aff.reshape(b, N_EXPERTS, nt, LANES)
    if nt < MIN_ROUTE_TILES:
        aff_tiles = jnp.pad(aff_tiles, ((0, 0), (0, 0), (0, MIN_ROUTE_TILES - nt), (0, 0)))
    idx, gate = _route(aff_tiles, cap)
    idx_flat = idx.reshape(-1)
    xg = _gather_rows(idx_flat, h2, cap)
    gate_e = jnp.swapaxes(gate.reshape(b, N_EXPERTS, cap), 0, 1).reshape(N_EXPERTS, b * cap, 1)
    y = _expert_ffn(xg, w_gate, w_up, w_down, gate_e, layer)
    return _combine_rows(idx_flat, y, b, t, cap)


def kernel(x, c, ctx, c_ctx, w_mod, b_mod, w_in, q_norm_a, k_norm_a, w_fourier, b_fourier, sink_c, w_out, ln1_g,
           ln1_b, w_router, w_gate, w_up, w_down, ln2_g, ln2_b):
    b, t, d = x.shape
    cos, sin = _rope_tables(t)
    gq, gk = _block_ones(A_Q), _block_ones(A_KV)
    cc = jnp.zeros((MOD_ROWS, d), F32).at[:b].set(c).at[b].set(c_ctx)
    mod = _modulation(cc, w_mod, b_mod)
    w_in_bf = w_in.astype(BF16)
    w_out_bf = w_out.astype(BF16)

    x_lat, x_ctx = x, ctx
    for layer in range(DEPTH):
        update_ctx = layer < DEPTH - 1
        lat = [mod[layer, :b, i * d:(i + 1) * d][:, None, :] for i in range(6)]
        cm = [jnp.broadcast_to(mod[layer, b, i * d:(i + 1) * d][None, None, :], (b, 1, d)) for i in range(6)]
        qn = jnp.tile(q_norm_a[layer], A_HEADS)[None, :]
        kn = jnp.tile(k_norm_a[layer], A_KV_HEADS)[None, :]
        w_four = jax.scipy.linalg.block_diag(*[w_fourier[layer, g] for g in range(B_GROUPS)])
        bias = b_fourier[layer].reshape(1, B_W)
        lng1, lnb1 = ln1_g[layer][None, :], ln1_b[layer][None, :]
        lng2, lnb2 = ln2_g[layer][None, :], ln2_b[layer][None, :]
        wr_t = w_router[layer].T
        sink = sink_c[layer]

        proj_ctx = _in_projection(x_ctx, cm[0], cm[1], w_in_bf[layer], qn, kn, gq, gk, cos, sin, rope=False)
        qa_c, qc_c, u_c, ka_c, va_c, kc_c, vc_c = proj_ctx
        qa, qc, u, ka, va, kc, vc = _in_projection(x_lat, lat[0], lat[1], w_in_bf[layer], qn, kn, gq, gk, cos, sin,
                                                   rope=True)
        out_a = _attention(qa, ka_c, va_c, ka, va, heads_per_kv=4, mode="full")
        out_b = _fourier_mix(u, w_four, bias)
        out_c = _attention(qc, kc_c, vc_c, kc, vc, sink, heads_per_kv=2, mode="window")
        x1, h2, aff = _out_projection(out_a, out_b, out_c, w_out_bf[layer], x_lat, lat[2], lat[3], lat[4],
                                      lng1, lnb1, wr_t)
        y = _moe(h2, aff, w_gate, w_up, w_down, layer)
        x_lat = _post_norm(x1, y, lat[5], lng2, lnb2)

        if update_ctx:
            out_a_c = _attention(qa_c, ka_c, va_c, heads_per_kv=4, mode="none")
            out_b_c = _fourier_mix(u_c, w_four, bias)
            out_c_c = _attention(qc_c, kc_c, vc_c, sink=sink, heads_per_kv=2, mode="none")
            x1c, h2c, affc = _out_projection(out_a_c, out_b_c, out_c_c, w_out_bf[layer], x_ctx, cm[2], cm[3], cm[4],
                                             lng1, lnb1, wr_t)
            yc = _moe(h2c, affc, w_gate, w_up, w_down, layer)
            x_ctx = _post_norm(x1c, yc, cm[5], lng2, lnb2)
    return x_lat
```

```python
import functools

import numpy as np
import jax
import jax.numpy as jnp
from jax import lax
from jax.experimental import pallas as pl
from jax.experimental.pallas import tpu as pltpu

F32 = jnp.float32
BF16 = jnp.bfloat16

D_MODEL = 1024
DEPTH = 2
GRID_W = 64
HEAD_DIM = 64
ROPE_BASE = 10000.0
A_HEADS, A_KV_HEADS = 8, 2
B_GROUPS, B_GROUP_DIM = 4, 64
C_HEADS, C_KV_HEADS = 4, 2
WINDOW = 128
N_EXPERTS = 16
EC_CAPACITY_FACTOR = 2
EXPERT_FF = 2 * D_MODEL
A_Q, A_KV, B_W, C_Q, C_KV = 512, 128, 256, 256, 128
QU_WIDTH = A_Q + C_Q + B_W
IN_WIDTH = QU_WIDTH + 2 * A_KV + 2 * C_KV
LN_EPS = 1e-5
RMS_EPS = 1e-6
NEG_INF = -1e30
LOG2E = 1.4426950408889634
SAFE_LOGIT = 60.0
DEEPNORM_ALPHA = (2 * DEPTH) ** 0.25
MOD_ROWS = 16
LANES = 128
ROW_SUBTILES = 4
WINDOW_Q_ROWS = 128
VMEM_LIMIT = 56 * 1024 * 1024

HIGHEST = lax.Precision.HIGHEST
NT_DIMS = (((1,), (1,)), ((), ()))


def _params(sem):
    return pltpu.CompilerParams(dimension_semantics=sem, vmem_limit_bytes=VMEM_LIMIT)


def _layer_norm(x):
    mu = jnp.mean(x, axis=-1, keepdims=True)
    xc = x - mu
    var = jnp.mean(xc * xc, axis=-1, keepdims=True)
    return xc * lax.rsqrt(var + LN_EPS)


def _silu(x):
    return x * (1.0 / (1.0 + jnp.exp(-x)))


def _mod_body(c_ref, w_ref, b_ref, o_ref):
    s = _silu(c_ref[...])
    o_ref[0] = jnp.dot(s, w_ref[0], precision=HIGHEST, preferred_element_type=F32) + b_ref[0]


def _modulation(cc, w_mod, b_mod):
    tn = 1024
    n = w_mod.shape[-1]
    return pl.pallas_call(
        _mod_body,
        grid=(DEPTH, n // tn),
        in_specs=[
            pl.BlockSpec((MOD_ROWS, D_MODEL), lambda l, j: (0, 0)),
            pl.BlockSpec((1, D_MODEL, tn), lambda l, j: (l, 0, j)),
            pl.BlockSpec((1, 1, tn), lambda l, j: (l, 0, j)),
        ],
        out_specs=pl.BlockSpec((1, MOD_ROWS, tn), lambda l, j: (l, 0, j)),
        out_shape=jax.ShapeDtypeStruct((DEPTH, MOD_ROWS, n), F32),
        compiler_params=_params(("arbitrary", "arbitrary")),
        name="modulation",
    )(cc, w_mod, b_mod.reshape(DEPTH, 1, n))


def _rope(x, cos, sin_signed, first_half):
    outs = []
    for j in range(x.shape[1] // LANES):
        xj = x[:, LANES * j:LANES * (j + 1)]
        partner = jnp.where(first_half, pltpu.roll(xj, LANES - 16, 1), pltpu.roll(xj, 16, 1))
        outs.append(xj * cos + partner * sin_signed)
    return jnp.concatenate(outs, axis=1) if len(outs) > 1 else outs[0]


def _dup_kv_heads(k, copies):
    lane = lax.broadcasted_iota(jnp.int32, (1, LANES), 1)
    lo = lane < HEAD_DIM
    r = pltpu.roll(k, HEAD_DIM, 1)
    d0 = jnp.where(lo, k, r)
    d1 = jnp.where(lo, r, k)
    reps = copies // 2
    return jnp.concatenate([d0] * reps + [d1] * reps, axis=1)


def _inproj_body(x_ref, sh_ref, sc_ref, w_ref, qn_ref, kn_ref, gq_ref, gk_ref, cos_ref, sin_ref,
                 qa_ref, qc_ref, u_ref, ka_ref, va_ref, kc_ref, vc_ref, *, rope):
    tm = x_ref.shape[1]
    sub = max(tm // ROW_SUBTILES, LANES)
    out_refs = (qa_ref, qc_ref, u_ref, ka_ref, va_ref, kc_ref, vc_ref)
    for i in range(tm // sub):
        rows = pl.ds(i * sub, sub)
        tables = (cos_ref[rows, :], sin_ref[rows, :]) if rope else None
        outs = _inproj_rows(x_ref[0, rows, :], sh_ref[0], sc_ref[0], w_ref, qn_ref[...], kn_ref[...], gq_ref, gk_ref,
                            tables)
        for ref, val in zip(out_refs, outs):
            ref[0, rows, :] = val


def _inproj_rows(x, sh, sc, w_ref, qn, kn, gq_ref, gk_ref, tables):
    h = _layer_norm(x) * (1.0 + sc) + sh
    p = jnp.dot(h.astype(BF16), w_ref[...], preferred_element_type=F32)
    qa = p[:, :A_Q]
    qc = p[:, A_Q:A_Q + C_Q]
    u = p[:, A_Q + C_Q:QU_WIDTH]
    ka = p[:, QU_WIDTH:QU_WIDTH + A_KV]
    va = p[:, QU_WIDTH + A_KV:QU_WIDTH + 2 * A_KV]
    kc = p[:, QU_WIDTH + 2 * A_KV:QU_WIDTH + 2 * A_KV + C_KV]
    vc = p[:, QU_WIDTH + 2 * A_KV + C_KV:]
    msq = jnp.dot((qa * qa).astype(BF16), gq_ref[...], preferred_element_type=F32) * (1.0 / HEAD_DIM)
    qa = qa * lax.rsqrt(msq + RMS_EPS) * qn
    msk = jnp.dot((ka * ka).astype(BF16), gk_ref[...], preferred_element_type=F32) * (1.0 / HEAD_DIM)
    ka = ka * lax.rsqrt(msk + RMS_EPS) * kn
    if tables is not None:
        cos, sin = tables
        lane = lax.broadcasted_iota(jnp.int32, (1, LANES), 1)
        first_half = (lane % 32) < 16
        qa = _rope(qa, cos, sin, first_half)
        qc = _rope(qc, cos, sin, first_half)
        ka = _rope(ka, cos, sin, first_half)
        kc = _rope(kc, cos, sin, first_half)
    scale = HEAD_DIM ** -0.5 * LOG2E
    return ((qa * scale).astype(BF16), (qc * scale).astype(BF16), u,
            _dup_kv_heads(ka, A_HEADS // A_KV_HEADS).astype(BF16),
            _dup_kv_heads(va, A_HEADS // A_KV_HEADS).astype(BF16),
            _dup_kv_heads(kc, C_HEADS // C_KV_HEADS).astype(BF16),
            _dup_kv_heads(vc, C_HEADS // C_KV_HEADS).astype(BF16))


def _in_projection(x, sh, sc, w_bf16, qn, kn, gq, gk, cos, sin, *, rope):
    b, t, d = x.shape
    tm = min(t, 512)
    row = lambda i, j: (i, j, 0)
    per_b = lambda i, j: (i, 0, 0)
    const = lambda i, j: (0, 0)
    tab = (lambda i, j: (j, 0)) if rope else const
    outs = [(A_Q, BF16), (C_Q, BF16), (B_W, F32), (4 * A_KV, BF16), (4 * A_KV, BF16),
            (2 * C_KV, BF16), (2 * C_KV, BF16)]
    return pl.pallas_call(
        functools.partial(_inproj_body, rope=rope),
        grid=(b, t // tm),
        in_specs=[
            pl.BlockSpec((1, tm, d), row),
            pl.BlockSpec((1, 1, d), per_b),
            pl.BlockSpec((1, 1, d), per_b),
            pl.BlockSpec((d, IN_WIDTH), const),
            pl.BlockSpec((1, A_Q), const),
            pl.BlockSpec((1, A_KV), const),
            pl.BlockSpec((A_Q, A_Q), const),
            pl.BlockSpec((A_KV, A_KV), const),
            pl.BlockSpec((tm, LANES), tab),
            pl.BlockSpec((tm, LANES), tab),
        ],
        out_specs=[pl.BlockSpec((1, tm, w), row) for w, _ in outs],
        out_shape=[jax.ShapeDtypeStruct((b, t, w), dt) for w, dt in outs],
        compiler_params=_params(("arbitrary", "arbitrary")),
        name="in_projection",
    )(x, sh, sc, w_bf16, qn, kn, gq, gk, cos, sin)


def _attn_body(*refs, heads_per_kv, mode, has_sink, tq, t_lat, kc_full):
    refs = list(refs)
    q_ref, kctx_ref, vctx_ref = refs[:3]
    pos = 3
    klat_ref = vlat_ref = sink_ref = None
    if mode != "none":
        klat_ref, vlat_ref = refs[pos:pos + 2]
        pos += 2
    if has_sink:
        sink_ref = refs[pos]
        pos += 1
    o_ref, kmax_ref = refs[pos:pos + 2]

    width = heads_per_kv * HEAD_DIM
    kvh = pl.program_id(1)
    qi = pl.program_id(2)
    q = q_ref[0]
    lane = lax.broadcasted_iota(jnp.int32, (1, width), 1)
    head_lanes = [(lane // HEAD_DIM) == g for g in range(heads_per_kv)]
    sinks = [sink_ref[kvh * heads_per_kv + g] * LOG2E for g in range(heads_per_kv)] if has_sink else None
    sub = min(tq, WINDOW_Q_ROWS) if mode == "window" else tq

    def per_head_lanes(cols):
        full = jnp.broadcast_to(cols[0], (sub, width))
        for g in range(1, heads_per_kv):
            full = jnp.where(head_lanes[g], cols[g], full)
        return full

    def key_sq_norm_max(ref):
        rows = ref.shape[1]
        step_rows = min(rows, 1024)

        def body(c, best):
            kf = ref[0, pl.ds(pl.multiple_of(c * step_rows, step_rows), step_rows), :].astype(F32)
            return jnp.maximum(best, jnp.max(jnp.sum(kf * kf, axis=1, keepdims=True), axis=0, keepdims=True))
        best = lax.fori_loop(0, rows // step_rows, body, jnp.zeros((1, 1), F32))
        return best[0, 0] * (1.0 / heads_per_kv)

    @pl.when(qi == 0)
    def _():
        best = key_sq_norm_max(kctx_ref)
        if mode != "none":
            best = jnp.maximum(best, key_sq_norm_max(klat_ref))
        kmax_ref[0] = best

    def sweep(stabilised):
        groups = [sweep_rows(stabilised, r0) for r0 in range(0, tq, sub)]
        return jnp.concatenate(groups, axis=0) if len(groups) > 1 else groups[0]

    def sweep_rows(stabilised, r0):
        q_rows = q[r0:r0 + sub]
        q_stack = jnp.concatenate([jnp.where(hl, q_rows, jnp.zeros_like(q_rows)) for hl in head_lanes], axis=0)

        def by_head(stacked):
            return [stacked[g * sub:(g + 1) * sub] for g in range(heads_per_kv)]

        def step(carry, k, v, mask=None):
            m, l, acc = carry
            s = lax.dot_general(q_stack, k, NT_DIMS, preferred_element_type=F32)
            if mask is not None:
                s = jnp.where(jnp.concatenate([mask] * heads_per_kv, axis=0), s, NEG_INF)
            if stabilised:
                m_new = jnp.maximum(m, jnp.max(s, axis=1, keepdims=True))
                alpha = jnp.exp2(m - m_new)
                p = jnp.exp2(s - m_new)
                l = alpha * l + jnp.sum(p, axis=1, keepdims=True)
                acc = acc * per_head_lanes(by_head(alpha))
            else:
                m_new = m
                p = jnp.exp2(s)
                l = l + jnp.sum(p, axis=1, keepdims=True)
            pv = jnp.dot(p.astype(BF16), v, preferred_element_type=F32)
            for g, part in enumerate(by_head(pv)):
                acc = acc + jnp.where(head_lanes[g], part, 0.0)
            return m_new, l, acc

        rows_of = lambda vals: jnp.concatenate([jnp.full((sub, 1), v, F32) for v in vals], axis=0)
        if not has_sink:
            m0 = jnp.full((heads_per_kv * sub, 1), NEG_INF if stabilised else 0.0, F32)
            l0 = jnp.zeros((heads_per_kv * sub, 1), F32)
        elif stabilised:
            m0 = rows_of(sinks)
            l0 = jnp.ones((heads_per_kv * sub, 1), F32)
        else:
            m0 = jnp.zeros((heads_per_kv * sub, 1), F32)
            l0 = jnp.exp2(rows_of(sinks))
        carry = (m0, l0, jnp.zeros((sub, width), F32))
        carry = step(carry, kctx_ref[0], vctx_ref[0])
        if mode == "full":
            for c in range(t_lat // kc_full):
                rows = pl.ds(c * kc_full, kc_full)
                carry = step(carry, klat_ref[0, rows, :], vlat_ref[0, rows, :])
        elif mode == "window":
            span = sub + 2 * WINDOW
            q0 = qi * tq + r0
            k0 = pl.multiple_of(jnp.clip(q0 - WINDOW, 0, t_lat - span), WINDOW)
            qpos = q0 + lax.broadcasted_iota(jnp.int32, (sub, 1), 0)
            kpos = k0 + lax.broadcasted_iota(jnp.int32, (1, span), 1)
            in_window = jnp.abs(kpos - qpos) <= WINDOW
            carry = step(carry, klat_ref[0, pl.ds(k0, span), :], vlat_ref[0, pl.ds(k0, span), :], in_window)
        _, l, acc = carry
        return acc * per_head_lanes(by_head(1.0 / l))

    qf = q.astype(F32)
    qf = qf * qf
    head_sq = [jnp.sum(jnp.where(hl, qf, 0.0), axis=1, keepdims=True) for hl in head_lanes]
    q_sq = jnp.max(functools.reduce(jnp.maximum, head_sq), axis=0, keepdims=True)[0, 0]
    bounded = q_sq * kmax_ref[0] <= SAFE_LOGIT * SAFE_LOGIT
    if has_sink:
        for g in range(heads_per_kv):
            bounded = jnp.logical_and(bounded, jnp.abs(sinks[g]) <= SAFE_LOGIT)
    out = lax.cond(bounded, lambda: sweep(False), lambda: sweep(True))
    o_ref[0] = out.astype(o_ref.dtype)


def _attention(q, kctx, vctx, klat=None, vlat=None, sink=None, *, heads_per_kv, mode):
    b, t, _ = q.shape
    n_kv = 2
    width = heads_per_kv * HEAD_DIM
    t_ctx = kctx.shape[1]
    tq = min(t, 512)
    qmap = lambda i, h, j: (i, j, h)
    kmap = lambda i, h, j: (i, 0, h)
    in_specs = [pl.BlockSpec((1, tq, width), qmap),
                pl.BlockSpec((1, t_ctx, width), kmap),
                pl.BlockSpec((1, t_ctx, width), kmap)]
    args = [q, kctx, vctx]
    t_lat = 0
    if mode != "none":
        t_lat = klat.shape[1]
        in_specs += [pl.BlockSpec((1, t_lat, width), kmap), pl.BlockSpec((1, t_lat, width), kmap)]
        args += [klat, vlat]
    if sink is not None:
        in_specs.append(pl.BlockSpec(memory_space=pltpu.SMEM))
        args.append(sink)
    body = functools.partial(_attn_body, heads_per_kv=heads_per_kv, mode=mode, has_sink=sink is not None,
                             tq=tq, t_lat=t_lat, kc_full=min(max(t_lat, 1), 1024))
    return pl.pallas_call(
        body,
        grid=(b, n_kv, t // tq),
        in_specs=in_specs,
        out_specs=pl.BlockSpec((1, tq, width), qmap),
        out_shape=jax.ShapeDtypeStruct((b, t, n_kv * width), BF16),
        scratch_shapes=[pltpu.SMEM((1,), F32)],
        compiler_params=_params(("arbitrary", "arbitrary", "arbitrary")),
        name=f"attention_{mode}_g{heads_per_kv}",
    )(*args)


def _split_bf16(x):
    hi = x.astype(BF16)
    return hi, (x - hi.astype(F32)).astype(BF16)


NN_DIMS = (((1,), (0,)), ((), ()))


def _dot3(a, b, dims=NN_DIMS):
    a_hi, a_lo = _split_bf16(a)
    b_hi, b_lo = _split_bf16(b)
    d = lambda x, y: lax.dot_general(x, y, dims, preferred_element_type=F32)
    return d(a_hi, b_hi) + (d(a_hi, b_lo) + d(a_lo, b_hi))


def _four_chan_body(u_ref, wblk_ref, cc_ref, sc_ref, z_ref):
    w = wblk_ref[...]
    wc = jnp.dot(cc_ref[...], w, precision=HIGHEST, preferred_element_type=F32)
    ws = jnp.dot(sc_ref[...], w, precision=HIGHEST, preferred_element_type=F32)
    u = u_ref[0]
    z_ref[0, 0] = _dot3(u, wc)
    z_ref[0, 1] = -_dot3(u, ws)


def _four_rows_body(z_ref, m_ref, tc_ref, ts_ref, a_ref):
    n1 = z_ref.shape[2]
    z = jnp.concatenate([z_ref[0, 0], z_ref[0, 1]], axis=0)
    a = _dot3(m_ref[...], z)
    ar, ai = a[:n1], a[n1:]
    tc, ts = tc_ref[...], ts_ref[...]
    a_ref[0, 0] = ar * tc + ai * ts
    a_ref[0, 1] = ai * tc - ar * ts


def _four_cols_body(a_ref, cs_ref, bias_ref, y_ref):
    kb = a_ref.shape[2]
    for j in range(kb):
        rhs = jnp.concatenate([a_ref[0, 0, j], a_ref[0, 1, j]], axis=0)
        y_ref[0, :, j, :] = _dot3(cs_ref[...], rhs) + bias_ref[...]


def _dft_constants(t):
    n = int(round(t ** 0.5))
    assert n * n == t
    k = np.arange(n)
    ang = 2.0 * np.pi * np.outer(k, k) / n
    c, s = np.cos(ang), np.sin(ang)
    rows = np.block([[c, s], [-s, c]]).astype(np.float32)
    cols = np.concatenate([c, s], axis=1).astype(np.float32)
    tw = 2.0 * np.pi * np.outer(k, k) / t
    tc = np.repeat(np.cos(tw), B_W, axis=1).astype(np.float32)
    ts = np.repeat(np.sin(tw), B_W, axis=1).astype(np.float32)
    return n, rows, cols, tc, ts


def _channel_constants(t):
    k = np.arange(B_GROUP_DIM)
    ang = 2.0 * np.pi * np.outer(k, k) / B_GROUP_DIM
    norm = 1.0 / np.sqrt(float(t) * B_GROUP_DIM)
    eye = np.eye(B_GROUPS)
    cc = np.kron(eye, np.cos(ang) * norm).astype(np.float32)
    sc = np.kron(eye, np.sin(ang) * norm).astype(np.float32)
    return cc, sc


def _fourier_mix(u, w_blockdiag, bias):
    b, t, _ = u.shape
    n, rows, cols, tc, ts = _dft_constants(t)
    cc, sc = _channel_constants(t)
    tm = min(t, 1024)
    z = pl.pallas_call(
        _four_chan_body,
        grid=(b, t // tm),
        in_specs=[pl.BlockSpec((1, tm, B_W), lambda i, j: (i, j, 0)),
                  pl.BlockSpec((B_W, B_W), lambda i, j: (0, 0)),
                  pl.BlockSpec((B_W, B_W), lambda i, j: (0, 0)),
                  pl.BlockSpec((B_W, B_W), lambda i, j: (0, 0))],
        out_specs=pl.BlockSpec((1, 2, tm, B_W), lambda i, j: (i, 0, j, 0)),
        out_shape=jax.ShapeDtypeStruct((b, 2, t, B_W), F32),
        compiler_params=_params(("arbitrary", "arbitrary")),
        name="fourier_channels",
    )(u, w_blockdiag, jnp.asarray(cc), jnp.asarray(sc))
    cols_total = n * B_W
    tc_cols = min(cols_total, 4096)
    a = pl.pallas_call(
        _four_rows_body,
        grid=(b, cols_total // tc_cols),
        in_specs=[pl.BlockSpec((1, 2, n, tc_cols), lambda i, j: (i, 0, 0, j)),
                  pl.BlockSpec((2 * n, 2 * n), lambda i, j: (0, 0)),
                  pl.BlockSpec((n, tc_cols), lambda i, j: (0, j)),
                  pl.BlockSpec((n, tc_cols), lambda i, j: (0, j))],
        out_specs=pl.BlockSpec((1, 2, n, tc_cols), lambda i, j: (i, 0, 0, j)),
        out_shape=jax.ShapeDtypeStruct((b, 2, n, cols_total), F32),
        compiler_params=_params(("arbitrary", "arbitrary")),
        name="fourier_rows",
    )(z.reshape(b, 2, n, cols_total), jnp.asarray(rows), jnp.asarray(tc), jnp.asarray(ts))
    kb = 8
    y = pl.pallas_call(
        _four_cols_body,
        grid=(b, n // kb),
        in_specs=[pl.BlockSpec((1, 2, kb, n, B_W), lambda i, j: (i, 0, j, 0, 0)),
                  pl.BlockSpec((n, 2 * n), lambda i, j: (0, 0)),
                  pl.BlockSpec((1, B_W), lambda i, j: (0, 0))],
        out_specs=pl.BlockSpec((1, n, kb, B_W), lambda i, j: (i, 0, j, 0)),
        out_shape=jax.ShapeDtypeStruct((b, n, n, B_W), F32),
        compiler_params=_params(("arbitrary", "arbitrary")),
        name="fourier_cols",
    )(a.reshape(b, 2, n, n, B_W), jnp.asarray(cols), bias)
    return y.reshape(b, t, B_W)


def _outproj_body(a_ref, f_ref, c_ref, w_ref, x_ref, g1_ref, sh2_ref, sc2_ref, lng_ref, lnb_ref, wr_ref,
                  x1_ref, h2_ref, aff_ref):
    tm = x_ref.shape[1]
    sub = max(tm // ROW_SUBTILES, LANES)
    for i in range(tm // sub):
        rows = pl.ds(i * sub, sub)
        o = jnp.dot(a_ref[0, rows, :], w_ref[:A_Q], preferred_element_type=F32)
        o = o + jnp.dot(f_ref[0, rows, :].astype(BF16), w_ref[A_Q:A_Q + B_W], preferred_element_type=F32)
        o = o + jnp.dot(c_ref[0, rows, :], w_ref[A_Q + B_W:], preferred_element_type=F32)
        x1 = _layer_norm(DEEPNORM_ALPHA * x_ref[0, rows, :] + g1_ref[0] * o) * lng_ref[...] + lnb_ref[...]
        x1_ref[0, rows, :] = x1
        h2 = _layer_norm(x1) * (1.0 + sc2_ref[0]) + sh2_ref[0]
        h2_ref[0, rows, :] = h2
        logits = _dot3(wr_ref[...], h2, NT_DIMS)
        e = jnp.exp(logits - jnp.max(logits, axis=0, keepdims=True))
        aff_ref[0, :, rows] = e / jnp.sum(e, axis=0, keepdims=True)


def _out_projection(oa, of, oc, w_bf16, x, g1, sh2, sc2, lng, lnb, wr_t):
    b, t, d = x.shape
    tm = min(t, 512)
    row = lambda i, j: (i, j, 0)
    per_b = lambda i, j: (i, 0, 0)
    const = lambda i, j: (0, 0)
    return pl.pallas_call(
        _outproj_body,
        grid=(b, t // tm),
        in_specs=[pl.BlockSpec((1, tm, A_Q), row),
                  pl.BlockSpec((1, tm, B_W), row),
                  pl.BlockSpec((1, tm, C_Q), row),
                  pl.BlockSpec((d, d), const),
                  pl.BlockSpec((1, tm, d), row),
                  pl.BlockSpec((1, 1, d), per_b),
                  pl.BlockSpec((1, 1, d), per_b),
                  pl.BlockSpec((1, 1, d), per_b),
                  pl.BlockSpec((1, d), const),
                  pl.BlockSpec((1, d), const),
                  pl.BlockSpec((N_EXPERTS, d), const)],
        out_specs=[pl.BlockSpec((1, tm, d), row),
                   pl.BlockSpec((1, tm, d), row),
                   pl.BlockSpec((1, N_EXPERTS, tm), lambda i, j: (i, 0, j))],
        out_shape=[jax.ShapeDtypeStruct((b, t, d), F32),
                   jax.ShapeDtypeStruct((b, t, d), F32),
                   jax.ShapeDtypeStruct((b, N_EXPERTS, t), F32)],
        compiler_params=_params(("arbitrary", "arbitrary")),
        name="out_projection",
    )(oa, of, oc, w_bf16, x, g1, sh2, sc2, lng, lnb, wr_t)


def _ffn_body(x_ref, wg_ref, wu_ref, wd_ref, gate_ref, o_ref):
    @pl.when(pl.program_id(2) == 0)
    def _():
        o_ref[...] = jnp.zeros_like(o_ref)

    x = x_ref[0]
    a = jnp.dot(x, wg_ref[0, 0].astype(BF16), preferred_element_type=F32)
    up = jnp.dot(x, wu_ref[0, 0].astype(BF16), preferred_element_type=F32)
    h = (_silu(a) * up).astype(BF16)
    y = jnp.dot(h, wd_ref[0, 0].astype(BF16), preferred_element_type=F32)
    o_ref[0] += gate_ref[0] * y


def _expert_ffn(xg, w_gate, w_up, w_down, gate, layer):
    e, m, d = xg.shape
    ff = w_gate.shape[-1]
    tm = min(m, 2048)
    tf = 512
    return pl.pallas_call(
        _ffn_body,
        grid=(e, m // tm, ff // tf),
        in_specs=[pl.BlockSpec((1, tm, d), lambda i, j, k: (i, j, 0)),
                  pl.BlockSpec((1, 1, d, tf), lambda i, j, k: (layer, i, 0, k)),
                  pl.BlockSpec((1, 1, d, tf), lambda i, j, k: (layer, i, 0, k)),
                  pl.BlockSpec((1, 1, tf, d), lambda i, j, k: (layer, i, k, 0)),
                  pl.BlockSpec((1, tm, 1), lambda i, j, k: (i, j, 0))],
        out_specs=pl.BlockSpec((1, tm, d), lambda i, j, k: (i, j, 0)),
        out_shape=jax.ShapeDtypeStruct((e, m, d), F32),
        compiler_params=_params(("arbitrary", "arbitrary", "arbitrary")),
        name="expert_ffn",
    )(xg, w_gate, w_up, w_down, gate)


def _route_body(aff_ref, tri_ref, lower_ref, idx_ref, gate_ref, cum_ref, *, cap):
    nb, n_exp, nt, _ = aff_ref.shape
    lists = nb * n_exp
    rows = n_exp * nt
    a = aff_ref[...].reshape(lists, nt, LANES)

    def count(mask):
        ones = jnp.where(mask, 1.0, 0.0)
        return jnp.sum(jnp.sum(ones, axis=1, keepdims=True), axis=2, keepdims=True)

    def token_prefix(mask):
        flat = jnp.where(mask, 1.0, 0.0).astype(BF16).reshape(nb * rows, LANES)
        in_row = jnp.dot(flat, tri_ref[...], preferred_element_type=F32)
        row_total = jnp.broadcast_to(in_row[:, LANES - 1:LANES], in_row.shape).astype(BF16)
        before = [jnp.dot(lower_ref[...], row_total[i * rows:(i + 1) * rows], preferred_element_type=F32)
                  for i in range(nb)]
        return (in_row + jnp.concatenate(before, axis=0)).reshape(lists, nt, LANES)

    def enough(cand):
        return count(a >= cand) >= cap

    power = jnp.ones((lists, 1, 1), F32)
    for k in range(6, -1, -1):
        too_few = jnp.logical_not(enough(power * 2.0 ** -(2 ** k - 1)))
        power = jnp.where(too_few, power * 2.0 ** -(2 ** k), power)
    base = jnp.where(enough(power), power, 0.0)

    def refine(_, carry):
        thr, digit = carry
        digit = digit * 0.5
        cand = thr + digit
        return jnp.where(enough(cand), cand, thr), digit

    thr, _ = lax.fori_loop(0, 23, refine, (base, base))
    above = a > thr
    tied = a == thr
    need = cap - count(above)
    chosen = above | (tied & (token_prefix(tied) <= need))
    cum_ref[...] = token_prefix(chosen)

    slot = lax.broadcasted_iota(jnp.int32, (cap, 1), 0).astype(F32)
    tile_eye = lax.broadcasted_iota(jnp.int32, (nt, nt), 0) == lax.broadcasted_iota(jnp.int32, (nt, nt), 1)
    tile_id = lax.broadcasted_iota(jnp.int32, (cap, nt), 1).astype(F32)
    lane_id = lax.broadcasted_iota(jnp.int32, (cap, LANES), 1).astype(F32)

    blk = min(cap, LANES)
    blk_eye = lax.broadcasted_iota(jnp.int32, (blk, blk), 0) == lax.broadcasted_iota(jnp.int32, (blk, blk), 1)

    def as_row(col):
        parts = [jnp.sum(jnp.where(blk_eye, col[j * blk:(j + 1) * blk], 0.0), axis=0, keepdims=True)
                 for j in range(cap // blk)]
        return jnp.concatenate(parts, axis=1) if len(parts) > 1 else parts[0]

    def per_list(i, carry):
        cum = cum_ref[i]
        aff = aff_ref[i // n_exp, i % n_exp]
        tile_end = jnp.sum(jnp.where(tile_eye, cum[:, LANES - 1:LANES], 0.0), axis=0, keepdims=True)
        tile_of_slot = jnp.sum(jnp.where(tile_end <= slot, 1.0, 0.0), axis=1, keepdims=True)
        pick_tile = jnp.where(tile_id == tile_of_slot, 1.0, 0.0).astype(BF16)
        cum_hi = jnp.floor(cum * (1.0 / 16.0))
        g1 = aff.astype(BF16)
        r1 = aff - g1.astype(F32)
        g2 = r1.astype(BF16)
        g3 = (r1 - g2.astype(F32)).astype(BF16)
        table = jnp.concatenate([cum_hi.astype(BF16), (cum - 16.0 * cum_hi).astype(BF16), g1, g2, g3], axis=1)
        got = jnp.dot(pick_tile, table, preferred_element_type=F32)
        counts = 16.0 * got[:, :LANES] + got[:, LANES:2 * LANES]
        within = jnp.sum(jnp.where(counts <= slot, 1.0, 0.0), axis=1, keepdims=True)
        affs = (got[:, 2 * LANES:3 * LANES] + got[:, 3 * LANES:4 * LANES]) + got[:, 4 * LANES:]
        gate = jnp.sum(jnp.where(lane_id == within, affs, 0.0), axis=1, keepdims=True)
        idx_ref[pl.ds(i, 1), :] = as_row(LANES * tile_of_slot + within).astype(jnp.int32)
        gate_ref[pl.ds(i, 1), :] = as_row(gate)
        return carry

    lax.fori_loop(0, lists, per_list, 0)


def _route(aff, cap):
    b, n_exp, nt, _ = aff.shape
    rows = n_exp * nt
    r = np.arange(rows)
    same_expert = (r[:, None] // nt) == (r[None, :] // nt)
    lane = np.arange(LANES)
    tri = jnp.asarray((lane[:, None] <= lane[None, :]).astype(np.float32), dtype=BF16)
    lower = jnp.asarray((same_expert & (r[None, :] < r[:, None])).astype(np.float32), dtype=BF16)
    whole = lambda i: (0, 0, 0, 0)
    return pl.pallas_call(
        functools.partial(_route_body, cap=cap),
        grid=(1,),
        in_specs=[pl.BlockSpec((b, n_exp, nt, LANES), whole),
                  pl.BlockSpec((LANES, LANES), lambda i: (0, 0)),
                  pl.BlockSpec((rows, rows), lambda i: (0, 0))],
        out_specs=[pl.BlockSpec((b * n_exp, cap), lambda i: (0, 0)),
                   pl.BlockSpec((b * n_exp, cap), lambda i: (0, 0))],
        out_shape=[jax.ShapeDtypeStruct((b * n_exp, cap), jnp.int32),
                   jax.ShapeDtypeStruct((b * n_exp, cap), F32)],
        scratch_shapes=[pltpu.VMEM((b * n_exp, nt, LANES), F32)],
        compiler_params=_params(("arbitrary",)),
        name="route",
    )(aff, tri, lower)


ROW_UNROLL = 8


def _gather_body(idx_ref, h_ref, o_ref, rows_ref, *, cap):
    base = (pl.program_id(0) * pl.num_programs(1) + pl.program_id(1)) * cap

    def body(i, carry):
        for r in range(ROW_UNROLL):
            s = i * ROW_UNROLL + r
            rows_ref[pl.ds(s, 1), :] = h_ref[0, pl.ds(idx_ref[base + s], 1), :]
        return carry

    lax.fori_loop(0, cap // ROW_UNROLL, body, 0)
    o_ref[0] = rows_ref[...].astype(o_ref.dtype)


def _gather_rows(idx_flat, h, cap):
    b, t, d = h.shape
    return pl.pallas_call(
        functools.partial(_gather_body, cap=cap),
        grid_spec=pltpu.PrefetchScalarGridSpec(
            num_scalar_prefetch=1,
            grid=(b, N_EXPERTS),
            in_specs=[pl.BlockSpec((1, t, d), lambda i, e, idx: (i, 0, 0))],
            out_specs=pl.BlockSpec((1, cap, d), lambda i, e, idx: (e, i, 0)),
            scratch_shapes=[pltpu.VMEM((cap, d), F32)]),
        out_shape=jax.ShapeDtypeStruct((N_EXPERTS, b * cap, d), BF16),
        compiler_params=_params(("arbitrary", "arbitrary")),
        name="gather_rows",
    )(idx_flat, h)


def _combine_body(idx_ref, y_ref, o_ref, *, cap):
    e = pl.program_id(1)
    base = (pl.program_id(0) * pl.num_programs(1) + e) * cap

    @pl.when(e == 0)
    def _():
        o_ref[...] = jnp.zeros_like(o_ref)

    def body(i, carry):
        toks = [idx_ref[base + i * ROW_UNROLL + r] for r in range(ROW_UNROLL)]
        sums = [o_ref[0, pl.ds(toks[r], 1), :] + y_ref[0, pl.ds(i * ROW_UNROLL + r, 1), :] for r in range(ROW_UNROLL)]
        for r in range(ROW_UNROLL):
            o_ref[0, pl.ds(toks[r], 1), :] = sums[r]
        return carry

    lax.fori_loop(0, cap // ROW_UNROLL, body, 0)


def _combine_rows(idx_flat, y, b, t, cap):
    d = y.shape[-1]
    return pl.pallas_call(
        functools.partial(_combine_body, cap=cap),
        grid_spec=pltpu.PrefetchScalarGridSpec(
            num_scalar_prefetch=1,
            grid=(b, N_EXPERTS),
            in_specs=[pl.BlockSpec((1, cap, d), lambda i, e, idx: (e, i, 0))],
            out_specs=pl.BlockSpec((1, t, d), lambda i, e, idx: (i, 0, 0))),
        out_shape=jax.ShapeDtypeStruct((b, t, d), F32),
        compiler_params=_params(("arbitrary", "arbitrary")),
        name="combine_rows",
    )(idx_flat, y)


def _postnorm_body(x_ref, y_ref, g_ref, lng_ref, lnb_ref, o_ref):
    o_ref[0] = _layer_norm(DEEPNORM_ALPHA * x_ref[0] + g_ref[0] * y_ref[0]) * lng_ref[...] + lnb_ref[...]


def _post_norm(x, y, g, lng, lnb):
    b, t, d = x.shape
    tm = min(t, 1024)
    row = lambda i, j: (i, j, 0)
    return pl.pallas_call(
        _postnorm_body,
        grid=(b, t // tm),
        in_specs=[pl.BlockSpec((1, tm, d), row), pl.BlockSpec((1, tm, d), row),
                  pl.BlockSpec((1, 1, d), lambda i, j: (i, 0, 0)),
                  pl.BlockSpec((1, d), lambda i, j: (0, 0)), pl.BlockSpec((1, d), lambda i, j: (0, 0))],
        out_specs=pl.BlockSpec((1, tm, d), row),
        out_shape=jax.ShapeDtypeStruct((b, t, d), F32),
        compiler_params=_params(("arbitrary", "arbitrary")),
        name="post_norm",
    )(x, y, g, lng, lnb)


def _rope_tables(t):
    tok = jnp.arange(t, dtype=jnp.int32)
    lane = np.arange(LANES)
    d = lane % HEAD_DIM
    inv_freq = ROPE_BASE ** (-jnp.arange(0, HEAD_DIM // 2, 2, dtype=F32) / (HEAD_DIM // 2))
    freq = inv_freq[d % 16]
    use_col = jnp.asarray((d // 32) == 1)
    position = jnp.where(use_col[None, :], (tok % GRID_W)[:, None], (tok // GRID_W)[:, None])
    ang = position * freq[None, :]
    sign = jnp.asarray(np.where((d % 32) < 16, -1.0, 1.0).astype(np.float32))
    return jnp.cos(ang).astype(F32), (jnp.sin(ang) * sign[None, :]).astype(F32)


def _block_ones(width):
    idx = np.arange(width) // HEAD_DIM
    return jnp.asarray((idx[:, None] == idx[None, :]).astype(np.float32), dtype=BF16)


MIN_ROUTE_TILES = 16


def _moe(h2, aff, w_gate, w_up, w_down, layer):
    b, t, _ = h2.shape
    cap = EC_CAPACITY_FACTOR * t // N_EXPERTS
    nt = t // LANES
    aff_tiles = aff.reshape(b, N_EXPERTS, nt, LANES)
    if nt < MIN_ROUTE_TILES:
        aff_tiles = jnp.pad(aff_tiles, ((0, 0), (0, 0), (0, MIN_ROUTE_TILES - nt), (0, 0)))
    idx, gate = _route(aff_tiles, cap)
    idx_flat = idx.reshape(-1)
    xg = _gather_rows(idx_flat, h2, cap)
    gate_e = jnp.swapaxes(gate.reshape(b, N_EXPERTS, cap), 0, 1).reshape(N_EXPERTS, b * cap, 1)
    y = _expert_ffn(xg, w_gate, w_up, w_down, gate_e, layer)
    return _combine_rows(idx_flat, y, b, t, cap)


def kernel(x, c, ctx, c_ctx, w_mod, b_mod, w_in, q_norm_a, k_norm_a, w_fourier, b_fourier, sink_c, w_out, ln1_g,
           ln1_b, w_router, w_gate, w_up, w_down, ln2_g, ln2_b):
    b, t, d = x.shape
    cos, sin = _rope_tables(t)
    gq, gk = _block_ones(A_Q), _block_ones(A_KV)
    cc = jnp.zeros((MOD_ROWS, d), F32).at[:b].set(c).at[b].set(c_ctx)
    mod = _modulation(cc, w_mod, b_mod)
    w_in_bf = w_in.astype(BF16)
    w_out_bf = w_out.astype(BF16)

    x_lat, x_ctx = x, ctx
    for layer in range(DEPTH):
        update_ctx = layer < DEPTH - 1
        lat = [mod[layer, :b, i * d:(i + 1) * d][:, None, :] for i in range(6)]
        cm = [jnp.broadcast_to(mod[layer, b, i * d:(i + 1) * d][None, None, :], (b, 1, d)) for i in range(6)]
        qn = jnp.tile(q_norm_a[layer], A_HEADS)[None, :]
        kn = jnp.tile(k_norm_a[layer], A_KV_HEADS)[None, :]
        w_four = jax.scipy.linalg.block_diag(*[w_fourier[layer, g] for g in range(B_GROUPS)])
        bias = b_fourier[layer].reshape(1, B_W)
        lng1, lnb1 = ln1_g[layer][None, :], ln1_b[layer][None, :]
        lng2, lnb2 = ln2_g[layer][None, :], ln2_b[layer][None, :]
        wr_t = w_router[layer].T
        sink = sink_c[layer]

        proj_ctx = _in_projection(x_ctx, cm[0], cm[1], w_in_bf[layer], qn, kn, gq, gk, cos, sin, rope=False)
        qa_c, qc_c, u_c, ka_c, va_c, kc_c, vc_c = proj_ctx
        qa, qc, u, ka, va, kc, vc = _in_projection(x_lat, lat[0], lat[1], w_in_bf[layer], qn, kn, gq, gk, cos, sin,
                                                   rope=True)
        out_a = _attention(qa, ka_c, va_c, ka, va, heads_per_kv=4, mode="full")
        out_b = _fourier_mix(u, w_four, bias)
        out_c = _attention(qc, kc_c, vc_c, kc, vc, sink, heads_per_kv=2, mode="window")
        x1, h2, aff = _out_projection(out_a, out_b, out_c, w_out_bf[layer], x_lat, lat[2], lat[3], lat[4],
                                      lng1, lnb1, wr_t)
        y = _moe(h2, aff, w_gate, w_up, w_down, layer)
        x_lat = _post_norm(x1, y, lat[5], lng2, lnb2)

        if update_ctx:
            out_a_c = _attention(qa_c, ka_c, va_c, heads_per_kv=4, mode="none")
            out_b_c = _fourier_mix(u_c, w_four, bias)
            out_c_c = _attention(qc_c, kc_c, vc_c, sink=sink, heads_per_kv=2, mode="none")
            x1c, h2c, affc = _out_projection(out_a_c, out_b_c, out_c_c, w_out_bf[layer], x_ctx, cm[2], cm[3], cm[4],
                                             lng1, lnb1, wr_t)
            yc = _moe(h2c, affc, w_gate, w_up, w_down, layer)
            x_ctx = _post_norm(x1c, yc, cm[5], lng2, lnb2)
    return x_lat
```

```python
import functools

import numpy as np
import jax
import jax.numpy as jnp
from jax import lax
from jax.experimental import pallas as pl
from jax.experimental.pallas import tpu as pltpu

F32 = jnp.float32
BF16 = jnp.bfloat16

D_MODEL = 1024
DEPTH = 2
GRID_W = 64
HEAD_DIM = 64
ROPE_BASE = 10000.0
A_HEADS, A_KV_HEADS = 8, 2
B_GROUPS, B_GROUP_DIM = 4, 64
C_HEADS, C_KV_HEADS = 4, 2
WINDOW = 128
N_EXPERTS = 16
EC_CAPACITY_FACTOR = 2
EXPERT_FF = 2 * D_MODEL
A_Q, A_KV, B_W, C_Q, C_KV = 512, 128, 256, 256, 128
QU_WIDTH = A_Q + C_Q + B_W
IN_WIDTH = QU_WIDTH + 2 * A_KV + 2 * C_KV
LN_EPS = 1e-5
RMS_EPS = 1e-6
NEG_INF = -1e30
LOG2E = 1.4426950408889634
SAFE_LOGIT = 60.0
DEEPNORM_ALPHA = (2 * DEPTH) ** 0.25
MOD_ROWS = 16
LANES = 128
ROW_SUBTILES = 4
WINDOW_Q_ROWS = 128
VMEM_LIMIT = 56 * 1024 * 1024

HIGHEST = lax.Precision.HIGHEST
NT_DIMS = (((1,), (1,)), ((), ()))


def _params(sem):
    return pltpu.CompilerParams(dimension_semantics=sem, vmem_limit_bytes=VMEM_LIMIT)


def _layer_norm(x):
    mu = jnp.mean(x, axis=-1, keepdims=True)
    xc = x - mu
    var = jnp.mean(xc * xc, axis=-1, keepdims=True)
    return xc * lax.rsqrt(var + LN_EPS)


def _silu(x):
    return x * (1.0 / (1.0 + jnp.exp(-x)))


def _mod_body(c_ref, w_ref, b_ref, o_ref):
    s = _silu(c_ref[...])
    o_ref[0] = jnp.dot(s, w_ref[0], precision=HIGHEST, preferred_element_type=F32) + b_ref[0]


def _modulation(cc, w_mod, b_mod):
    tn = 1024
    n = w_mod.shape[-1]
    return pl.pallas_call(
        _mod_body,
        grid=(DEPTH, n // tn),
        in_specs=[
            pl.BlockSpec((MOD_ROWS, D_MODEL), lambda l, j: (0, 0)),
            pl.BlockSpec((1, D_MODEL, tn), lambda l, j: (l, 0, j)),
            pl.BlockSpec((1, 1, tn), lambda l, j: (l, 0, j)),
        ],
        out_specs=pl.BlockSpec((1, MOD_ROWS, tn), lambda l, j: (l, 0, j)),
        out_shape=jax.ShapeDtypeStruct((DEPTH, MOD_ROWS, n), F32),
        compiler_params=_params(("arbitrary", "arbitrary")),
        name="modulation",
    )(cc, w_mod, b_mod.reshape(DEPTH, 1, n))


def _rope(x, cos, sin_signed, first_half):
    outs = []
    for j in range(x.shape[1] // LANES):
        xj = x[:, LANES * j:LANES * (j + 1)]
        partner = jnp.where(first_half, pltpu.roll(xj, LANES - 16, 1), pltpu.roll(xj, 16, 1))
        outs.append(xj * cos + partner * sin_signed)
    return jnp.concatenate(outs, axis=1) if len(outs) > 1 else outs[0]


def _dup_kv_heads(k, copies):
    lane = lax.broadcasted_iota(jnp.int32, (1, LANES), 1)
    lo = lane < HEAD_DIM
    r = pltpu.roll(k, HEAD_DIM, 1)
    d0 = jnp.where(lo, k, r)
    d1 = jnp.where(lo, r, k)
    reps = copies // 2
    return jnp.concatenate([d0] * reps + [d1] * reps, axis=1)


def _inproj_body(x_ref, sh_ref, sc_ref, w_ref, qn_ref, kn_ref, gq_ref, gk_ref, cos_ref, sin_ref,
                 qa_ref, qc_ref, u_ref, ka_ref, va_ref, kc_ref, vc_ref, *, rope):
    tm = x_ref.shape[1]
    sub = max(tm // ROW_SUBTILES, LANES)
    out_refs = (qa_ref, qc_ref, u_ref, ka_ref, va_ref, kc_ref, vc_ref)
    for i in range(tm // sub):
        rows = pl.ds(i * sub, sub)
        tables = (cos_ref[rows, :], sin_ref[rows, :]) if rope else None
        outs = _inproj_rows(x_ref[0, rows, :], sh_ref[0], sc_ref[0], w_ref, qn_ref[...], kn_ref[...], gq_ref, gk_ref,
                            tables)
        for ref, val in zip(out_refs, outs):
            ref[0, rows, :] = val


def _inproj_rows(x, sh, sc, w_ref, qn, kn, gq_ref, gk_ref, tables):
    h = _layer_norm(x) * (1.0 + sc) + sh
    p = jnp.dot(h.astype(BF16), w_ref[...], preferred_element_type=F32)
    qa = p[:, :A_Q]
    qc = p[:, A_Q:A_Q + C_Q]
    u = p[:, A_Q + C_Q:QU_WIDTH]
    ka = p[:, QU_WIDTH:QU_WIDTH + A_KV]
    va = p[:, QU_WIDTH + A_KV:QU_WIDTH + 2 * A_KV]
    kc = p[:, QU_WIDTH + 2 * A_KV:QU_WIDTH + 2 * A_KV + C_KV]
    vc = p[:, QU_WIDTH + 2 * A_KV + C_KV:]
    msq = jnp.dot((qa * qa).astype(BF16), gq_ref[...], preferred_element_type=F32) * (1.0 / HEAD_DIM)
    qa = qa * lax.rsqrt(msq + RMS_EPS) * qn
    msk = jnp.dot((ka * ka).astype(BF16), gk_ref[...], preferred_element_type=F32) * (1.0 / HEAD_DIM)
    ka = ka * lax.rsqrt(msk + RMS_EPS) * kn
    if tables is not None:
        cos, sin = tables
        lane = lax.broadcasted_iota(jnp.int32, (1, LANES), 1)
        first_half = (lane % 32) < 16
        qa = _rope(qa, cos, sin, first_half)
        qc = _rope(qc, cos, sin, first_half)
        ka = _rope(ka, cos, sin, first_half)
        kc = _rope(kc, cos, sin, first_half)
    scale = HEAD_DIM ** -0.5 * LOG2E
    return ((qa * scale).astype(BF16), (qc * scale).astype(BF16), u,
            _dup_kv_heads(ka, A_HEADS // A_KV_HEADS).astype(BF16),
            _dup_kv_heads(va, A_HEADS // A_KV_HEADS).astype(BF16),
            _dup_kv_heads(kc, C_HEADS // C_KV_HEADS).astype(BF16),
            _dup_kv_heads(vc, C_HEADS // C_KV_HEADS).astype(BF16))


def _in_projection(x, sh, sc, w_bf16, qn, kn, gq, gk, cos, sin, *, rope):
    b, t, d = x.shape
    tm = min(t, 512)
    row = lambda i, j: (i, j, 0)
    per_b = lambda i, j: (i, 0, 0)
    const = lambda i, j: (0, 0)
    tab = (lambda i, j: (j, 0)) if rope else const
    outs = [(A_Q, BF16), (C_Q, BF16), (B_W, F32), (4 * A_KV, BF16), (4 * A_KV, BF16),
            (2 * C_KV, BF16), (2 * C_KV, BF16)]
    return pl.pallas_call(
        functools.partial(_inproj_body, rope=rope),
        grid=(b, t // tm),
        in_specs=[
            pl.BlockSpec((1, tm, d), row),
            pl.BlockSpec((1, 1, d), per_b),
            pl.BlockSpec((1, 1, d), per_b),
            pl.BlockSpec((d, IN_WIDTH), const),
            pl.BlockSpec((1, A_Q), const),
            pl.BlockSpec((1, A_KV), const),
            pl.BlockSpec((A_Q, A_Q), const),
            pl.BlockSpec((A_KV, A_KV), const),
            pl.BlockSpec((tm, LANES), tab),
            pl.BlockSpec((tm, LANES), tab),
        ],
        out_specs=[pl.BlockSpec((1, tm, w), row) for w, _ in outs],
        out_shape=[jax.ShapeDtypeStruct((b, t, w), dt) for w, dt in outs],
        compiler_params=_params(("arbitrary", "arbitrary")),
        name="in_projection",
    )(x, sh, sc, w_bf16, qn, kn, gq, gk, cos, sin)


def _attn_body(*refs, heads_per_kv, mode, has_sink, tq, t_lat, kc_full):
    refs = list(refs)
    q_ref, kctx_ref, vctx_ref = refs[:3]
    pos = 3
    klat_ref = vlat_ref = sink_ref = None
    if mode != "none":
        klat_ref, vlat_ref = refs[pos:pos + 2]
        pos += 2
    if has_sink:
        sink_ref = refs[pos]
        pos += 1
    o_ref, kmax_ref = refs[pos:pos + 2]

    width = heads_per_kv * HEAD_DIM
    kvh = pl.program_id(1)
    qi = pl.program_id(2)
    q = q_ref[0]
    lane = lax.broadcasted_iota(jnp.int32, (1, width), 1)
    head_lanes = [(lane // HEAD_DIM) == g for g in range(heads_per_kv)]
    sinks = [sink_ref[kvh * heads_per_kv + g] * LOG2E for g in range(heads_per_kv)] if has_sink else None
    sub = min(tq, WINDOW_Q_ROWS) if mode == "window" else tq

    def per_head_lanes(cols):
        full = jnp.broadcast_to(cols[0], (sub, width))
        for g in range(1, heads_per_kv):
            full = jnp.where(head_lanes[g], cols[g], full)
        return full

    def key_sq_norm_max(ref):
        rows = ref.shape[1]
        step_rows = min(rows, 1024)

        def body(c, best):
            kf = ref[0, pl.ds(pl.multiple_of(c * step_rows, step_rows), step_rows), :].astype(F32)
            return jnp.maximum(best, jnp.max(jnp.sum(kf * kf, axis=1, keepdims=True), axis=0, keepdims=True))
        best = lax.fori_loop(0, rows // step_rows, body, jnp.zeros((1, 1), F32))
        return best[0, 0] * (1.0 / heads_per_kv)

    @pl.when(qi == 0)
    def _():
        best = key_sq_norm_max(kctx_ref)
        if mode != "none":
            best = jnp.maximum(best, key_sq_norm_max(klat_ref))
        kmax_ref[0] = best

    def sweep(stabilised):
        groups = [sweep_rows(stabilised, r0) for r0 in range(0, tq, sub)]
        return jnp.concatenate(groups, axis=0) if len(groups) > 1 else groups[0]

    def sweep_rows(stabilised, r0):
        q_rows = q[r0:r0 + sub]
        q_stack = jnp.concatenate([jnp.where(hl, q_rows, jnp.zeros_like(q_rows)) for hl in head_lanes], axis=0)

        def by_head(stacked):
            return [stacked[g * sub:(g + 1) * sub] for g in range(heads_per_kv)]

        def step(carry, k, v, mask=None):
            m, l, acc = carry
            s = lax.dot_general(q_stack, k, NT_DIMS, preferred_element_type=F32)
            if mask is not None:
                s = jnp.where(jnp.concatenate([mask] * heads_per_kv, axis=0), s, NEG_INF)
            if stabilised:
                m_new = jnp.maximum(m, jnp.max(s, axis=1, keepdims=True))
                alpha = jnp.exp2(m - m_new)
                p = jnp.exp2(s - m_new)
                l = alpha * l + jnp.sum(p, axis=1, keepdims=True)
                acc = acc * per_head_lanes(by_head(alpha))
            else:
                m_new = m
                p = jnp.exp2(s)
                l = l + jnp.sum(p, axis=1, keepdims=True)
            pv = jnp.dot(p.astype(BF16), v, preferred_element_type=F32)
            for g, part in enumerate(by_head(pv)):
                acc = acc + jnp.where(head_lanes[g], part, 0.0)
            return m_new, l, acc

        rows_of = lambda vals: jnp.concatenate([jnp.full((sub, 1), v, F32) for v in vals], axis=0)
        if not has_sink:
            m0 = jnp.full((heads_per_kv * sub, 1), NEG_INF if stabilised else 0.0, F32)
            l0 = jnp.zeros((heads_per_kv * sub, 1), F32)
        elif stabilised:
            m0 = rows_of(sinks)
            l0 = jnp.ones((heads_per_kv * sub, 1), F32)
        else:
            m0 = jnp.zeros((heads_per_kv * sub, 1), F32)
            l0 = jnp.exp2(rows_of(sinks))
        carry = (m0, l0, jnp.zeros((sub, width), F32))
        carry = step(carry, kctx_ref[0], vctx_ref[0])
        if mode == "full":
            for c in range(t_lat // kc_full):
                rows = pl.ds(c * kc_full, kc_full)
                carry = step(carry, klat_ref[0, rows, :], vlat_ref[0, rows, :])
        elif mode == "window":
            span = sub + 2 * WINDOW
            q0 = qi * tq + r0
            k0 = pl.multiple_of(jnp.clip(q0 - WINDOW, 0, t_lat - span), WINDOW)
            qpos = q0 + lax.broadcasted_iota(jnp.int32, (sub, 1), 0)
            kpos = k0 + lax.broadcasted_iota(jnp.int32, (1, span), 1)
            in_window = jnp.abs(kpos - qpos) <= WINDOW
            carry = step(carry, klat_ref[0, pl.ds(k0, span), :], vlat_ref[0, pl.ds(k0, span), :], in_window)
        _, l, acc = carry
        return acc * per_head_lanes(by_head(1.0 / l))

    qf = q.astype(F32)
    qf = qf * qf
    head_sq = [jnp.sum(jnp.where(hl, qf, 0.0), axis=1, keepdims=True) for hl in head_lanes]
    q_sq = jnp.max(functools.reduce(jnp.maximum, head_sq), axis=0, keepdims=True)[0, 0]
    bounded = q_sq * kmax_ref[0] <= SAFE_LOGIT * SAFE_LOGIT
    if has_sink:
        for g in range(heads_per_kv):
            bounded = jnp.logical_and(bounded, jnp.abs(sinks[g]) <= SAFE_LOGIT)
    out = lax.cond(bounded, lambda: sweep(False), lambda: sweep(True))
    o_ref[0] = out.astype(o_ref.dtype)


def _attention(q, kctx, vctx, klat=None, vlat=None, sink=None, *, heads_per_kv, mode):
    b, t, _ = q.shape
    n_kv = 2
    width = heads_per_kv * HEAD_DIM
    t_ctx = kctx.shape[1]
    tq = min(t, 512)
    qmap = lambda i, h, j: (i, j, h)
    kmap = lambda i, h, j: (i, 0, h)
    in_specs = [pl.BlockSpec((1, tq, width), qmap),
                pl.BlockSpec((1, t_ctx, width), kmap),
                pl.BlockSpec((1, t_ctx, width), kmap)]
    args = [q, kctx, vctx]
    t_lat = 0
    if mode != "none":
        t_lat = klat.shape[1]
        in_specs += [pl.BlockSpec((1, t_lat, width), kmap), pl.BlockSpec((1, t_lat, width), kmap)]
        args += [klat, vlat]
    if sink is not None:
        in_specs.append(pl.BlockSpec(memory_space=pltpu.SMEM))
        args.append(sink)
    body = functools.partial(_attn_body, heads_per_kv=heads_per_kv, mode=mode, has_sink=sink is not None,
                             tq=tq, t_lat=t_lat, kc_full=min(max(t_lat, 1), 1024))
    return pl.pallas_call(
        body,
        grid=(b, n_kv, t // tq),
        in_specs=in_specs,
        out_specs=pl.BlockSpec((1, tq, width), qmap),
        out_shape=jax.ShapeDtypeStruct((b, t, n_kv * width), BF16),
        scratch_shapes=[pltpu.SMEM((1,), F32)],
        compiler_params=_params(("arbitrary", "arbitrary", "arbitrary")),
        name=f"attention_{mode}_g{heads_per_kv}",
    )(*args)


def _split_bf16(x):
    hi = x.astype(BF16)
    return hi, (x - hi.astype(F32)).astype(BF16)


NN_DIMS = (((1,), (0,)), ((), ()))


def _dot3(a, b, dims=NN_DIMS):
    a_hi, a_lo = _split_bf16(a)
    b_hi, b_lo = _split_bf16(b)
    d = lambda x, y: lax.dot_general(x, y, dims, preferred_element_type=F32)
    return d(a_hi, b_hi) + (d(a_hi, b_lo) + d(a_lo, b_hi))


def _four_chan_body(u_ref, wblk_ref, cc_ref, sc_ref, z_ref):
    w = wblk_ref[...]
    wc = jnp.dot(cc_ref[...], w, precision=HIGHEST, preferred_element_type=F32)
    ws = jnp.dot(sc_ref[...], w, precision=HIGHEST, preferred_element_type=F32)
    u = u_ref[0]
    z_ref[0, 0] = _dot3(u, wc)
    z_ref[0, 1] = -_dot3(u, ws)


def _four_rows_body(z_ref, m_ref, tc_ref, ts_ref, a_ref):
    n1 = z_ref.shape[2]
    z = jnp.concatenate([z_ref[0, 0], z_ref[0, 1]], axis=0)
    a = _dot3(m_ref[...], z)
    ar, ai = a[:n1], a[n1:]
    tc, ts = tc_ref[...], ts_ref[...]
    a_ref[0, 0] = ar * tc + ai * ts
    a_ref[0, 1] = ai * tc - ar * ts


def _four_cols_body(a_ref, cs_ref, bias_ref, y_ref):
    kb = a_ref.shape[2]
    for j in range(kb):
        rhs = jnp.concatenate([a_ref[0, 0, j], a_ref[0, 1, j]], axis=0)
        y_ref[0, :, j, :] = _dot3(cs_ref[...], rhs) + bias_ref[...]


def _dft_constants(t):
    n = int(round(t ** 0.5))
    assert n * n == t
    k = np.arange(n)
    ang = 2.0 * np.pi * np.outer(k, k) / n
    c, s = np.cos(ang), np.sin(ang)
    rows = np.block([[c, s], [-s, c]]).astype(np.float32)
    cols = np.concatenate([c, s], axis=1).astype(np.float32)
    tw = 2.0 * np.pi * np.outer(k, k) / t
    tc = np.repeat(np.cos(tw), B_W, axis=1).astype(np.float32)
    ts = np.repeat(np.sin(tw), B_W, axis=1).astype(np.float32)
    return n, rows, cols, tc, ts


def _channel_constants(t):
    k = np.arange(B_GROUP_DIM)
    ang = 2.0 * np.pi * np.outer(k, k) / B_GROUP_DIM
    norm = 1.0 / np.sqrt(float(t) * B_GROUP_DIM)
    eye = np.eye(B_GROUPS)
    cc = np.kron(eye, np.cos(ang) * norm).astype(np.float32)
    sc = np.kron(eye, np.sin(ang) * norm).astype(np.float32)
    return cc, sc


def _fourier_mix(u, w_blockdiag, bias):
    b, t, _ = u.shape
    n, rows, cols, tc, ts = _dft_constants(t)
    cc, sc = _channel_constants(t)
    tm = min(t, 1024)
    z = pl.pallas_call(
        _four_chan_body,
        grid=(b, t // tm),
        in_specs=[pl.BlockSpec((1, tm, B_W), lambda i, j: (i, j, 0)),
                  pl.BlockSpec((B_W, B_W), lambda i, j: (0, 0)),
                  pl.BlockSpec((B_W, B_W), lambda i, j: (0, 0)),
                  pl.BlockSpec((B_W, B_W), lambda i, j: (0, 0))],
        out_specs=pl.BlockSpec((1, 2, tm, B_W), lambda i, j: (i, 0, j, 0)),
        out_shape=jax.ShapeDtypeStruct((b, 2, t, B_W), F32),
        compiler_params=_params(("arbitrary", "arbitrary")),
        name="fourier_channels",
    )(u, w_blockdiag, jnp.asarray(cc), jnp.asarray(sc))
    cols_total = n * B_W
    tc_cols = min(cols_total, 4096)
    a = pl.pallas_call(
        _four_rows_body,
        grid=(b, cols_total // tc_cols),
        in_specs=[pl.BlockSpec((1, 2, n, tc_cols), lambda i, j: (i, 0, 0, j)),
                  pl.BlockSpec((2 * n, 2 * n), lambda i, j: (0, 0)),
                  pl.BlockSpec((n, tc_cols), lambda i, j: (0, j)),
                  pl.BlockSpec((n, tc_cols), lambda i, j: (0, j))],
        out_specs=pl.BlockSpec((1, 2, n, tc_cols), lambda i, j: (i, 0, 0, j)),
        out_shape=jax.ShapeDtypeStruct((b, 2, n, cols_total), F32),
        compiler_params=_params(("arbitrary", "arbitrary")),
        name="fourier_rows",
    )(z.reshape(b, 2, n, cols_total), jnp.asarray(rows), jnp.asarray(tc), jnp.asarray(ts))
    kb = 8
    y = pl.pallas_call(
        _four_cols_body,
        grid=(b, n // kb),
        in_specs=[pl.BlockSpec((1, 2, kb, n, B_W), lambda i, j: (i, 0, j, 0, 0)),
                  pl.BlockSpec((n, 2 * n), lambda i, j: (0, 0)),
                  pl.BlockSpec((1, B_W), lambda i, j: (0, 0))],
        out_specs=pl.BlockSpec((1, n, kb, B_W), lambda i, j: (i, 0, j, 0)),
        out_shape=jax.ShapeDtypeStruct((b, n, n, B_W), F32),
        compiler_params=_params(("arbitrary", "arbitrary")),
        name="fourier_cols",
    )(a.reshape(b, 2, n, n, B_W), jnp.asarray(cols), bias)
    return y.reshape(b, t, B_W)


def _outproj_body(a_ref, f_ref, c_ref, w_ref, x_ref, g1_ref, sh2_ref, sc2_ref, lng_ref, lnb_ref, wr_ref,
                  x1_ref, h2_ref, aff_ref):
    tm = x_ref.shape[1]
    sub = max(tm // ROW_SUBTILES, LANES)
    for i in range(tm // sub):
        rows = pl.ds(i * sub, sub)
        o = jnp.dot(a_ref[0, rows, :], w_ref[:A_Q], preferred_element_type=F32)
        o = o + jnp.dot(f_ref[0, rows, :].astype(BF16), w_ref[A_Q:A_Q + B_W], preferred_element_type=F32)
        o = o + jnp.dot(c_ref[0, rows, :], w_ref[A_Q + B_W:], preferred_element_type=F32)
        x1 = _layer_norm(DEEPNORM_ALPHA * x_ref[0, rows, :] + g1_ref[0] * o) * lng_ref[...] + lnb_ref[...]
        x1_ref[0, rows, :] = x1
        h2 = _layer_norm(x1) * (1.0 + sc2_ref[0]) + sh2_ref[0]
        h2_ref[0, rows, :] = h2
        logits = _dot3(wr_ref[...], h2, NT_DIMS)
        e = jnp.exp(logits - jnp.max(logits, axis=0, keepdims=True))
        aff_ref[0, :, rows] = e / jnp.sum(e, axis=0, keepdims=True)


def _out_projection(oa, of, oc, w_bf16, x, g1, sh2, sc2, lng, lnb, wr_t):
    b, t, d = x.shape
    tm = min(t, 512)
    row = lambda i, j: (i, j, 0)
    per_b = lambda i, j: (i, 0, 0)
    const = lambda i, j: (0, 0)
    return pl.pallas_call(
        _outproj_body,
        grid=(b, t // tm),
        in_specs=[pl.BlockSpec((1, tm, A_Q), row),
                  pl.BlockSpec((1, tm, B_W), row),
                  pl.BlockSpec((1, tm, C_Q), row),
                  pl.BlockSpec((d, d), const),
                  pl.BlockSpec((1, tm, d), row),
                  pl.BlockSpec((1, 1, d), per_b),
                  pl.BlockSpec((1, 1, d), per_b),
                  pl.BlockSpec((1, 1, d), per_b),
                  pl.BlockSpec((1, d), const),
                  pl.BlockSpec((1, d), const),
                  pl.BlockSpec((N_EXPERTS, d), const)],
        out_specs=[pl.BlockSpec((1, tm, d), row),
                   pl.BlockSpec((1, tm, d), row),
                   pl.BlockSpec((1, N_EXPERTS, tm), lambda i, j: (i, 0, j))],
        out_shape=[jax.ShapeDtypeStruct((b, t, d), F32),
                   jax.ShapeDtypeStruct((b, t, d), F32),
                   jax.ShapeDtypeStruct((b, N_EXPERTS, t), F32)],
        compiler_params=_params(("arbitrary", "arbitrary")),
        name="out_projection",
    )(oa, of, oc, w_bf16, x, g1, sh2, sc2, lng, lnb, wr_t)


def _ffn_body(x_ref, wg_ref, wu_ref, wd_ref, gate_ref, o_ref):
    @pl.when(pl.program_id(2) == 0)
    def _():
        o_ref[...] = jnp.zeros_like(o_ref)

    x = x_ref[0]
    a = jnp.dot(x, wg_ref[0, 0].astype(BF16), preferred_element_type=F32)
    up = jnp.dot(x, wu_ref[0, 0].astype(BF16), preferred_element_type=F32)
    h = (_silu(a) * up).astype(BF16)
    y = jnp.dot(h, wd_ref[0, 0].astype(BF16), preferred_element_type=F32)
    o_ref[0] += gate_ref[0] * y


def _expert_ffn(xg, w_gate, w_up, w_down, gate, layer):
    e, m, d = xg.shape
    ff = w_gate.shape[-1]
    tm = min(m, 2048)
    tf = 512
    return pl.pallas_call(
        _ffn_body,
        grid=(e, m // tm, ff // tf),
        in_specs=[pl.BlockSpec((1, tm, d), lambda i, j, k: (i, j, 0)),
                  pl.BlockSpec((1, 1, d, tf), lambda i, j, k: (layer, i, 0, k)),
                  pl.BlockSpec((1, 1, d, tf), lambda i, j, k: (layer, i, 0, k)),
                  pl.BlockSpec((1, 1, tf, d), lambda i, j, k: (layer, i, k, 0)),
                  pl.BlockSpec((1, tm, 1), lambda i, j, k: (i, j, 0))],
        out_specs=pl.BlockSpec((1, tm, d), lambda i, j, k: (i, j, 0)),
        out_shape=jax.ShapeDtypeStruct((e, m, d), F32),
        compiler_params=_params(("arbitrary", "arbitrary", "arbitrary")),
        name="expert_ffn",
    )(xg, w_gate, w_up, w_down, gate)


def _route_body(aff_ref, tri_ref, lower_ref, idx_ref, gate_ref, cum_ref, *, cap):
    nb, n_exp, nt, _ = aff_ref.shape
    lists = nb * n_exp
    rows = n_exp * nt
    a = aff_ref[...].reshape(lists, nt, LANES)

    def count(mask):
        ones = jnp.where(mask, 1.0, 0.0)
        return jnp.sum(jnp.sum(ones, axis=1, keepdims=True), axis=2, keepdims=True)

    def token_prefix(mask):
        flat = jnp.where(mask, 1.0, 0.0).astype(BF16).reshape(nb * rows, LANES)
        in_row = jnp.dot(flat, tri_ref[...], preferred_element_type=F32)
        row_total = jnp.broadcast_to(in_row[:, LANES - 1:LANES], in_row.shape).astype(BF16)
        before = [jnp.dot(lower_ref[...], row_total[i * rows:(i + 1) * rows], preferred_element_type=F32)
                  for i in range(nb)]
        return (in_row + jnp.concatenate(before, axis=0)).reshape(lists, nt, LANES)

    def enough(cand):
        return count(a >= cand) >= cap

    power = jnp.ones((lists, 1, 1), F32)
    for k in range(6, -1, -1):
        too_few = jnp.logical_not(enough(power * 2.0 ** -(2 ** k - 1)))
        power = jnp.where(too_few, power * 2.0 ** -(2 ** k), power)
    base = jnp.where(enough(power), power, 0.0)

    def refine(_, carry):
        thr, digit = carry
        digit = digit * 0.5
        cand = thr + digit
        return jnp.where(enough(cand), cand, thr), digit

    thr, _ = lax.fori_loop(0, 23, refine, (base, base))
    above = a > thr
    tied = a == thr
    need = cap - count(above)
    chosen = above | (tied & (token_prefix(tied) <= need))
    cum_ref[...] = token_prefix(chosen)

    slot = lax.broadcasted_iota(jnp.int32, (cap, 1), 0).astype(F32)
    tile_eye = lax.broadcasted_iota(jnp.int32, (nt, nt), 0) == lax.broadcasted_iota(jnp.int32, (nt, nt), 1)
    tile_id = lax.broadcasted_iota(jnp.int32, (cap, nt), 1).astype(F32)
    lane_id = lax.broadcasted_iota(jnp.int32, (cap, LANES), 1).astype(F32)

    blk = min(cap, LANES)
    blk_eye = lax.broadcasted_iota(jnp.int32, (blk, blk), 0) == lax.broadcasted_iota(jnp.int32, (blk, blk), 1)

    def as_row(col):
        parts = [jnp.sum(jnp.where(blk_eye, col[j * blk:(j + 1) * blk], 0.0), axis=0, keepdims=True)
                 for j in range(cap // blk)]
        return jnp.concatenate(parts, axis=1) if len(parts) > 1 else parts[0]

    def per_list(i, carry):
        cum = cum_ref[i]
        aff = aff_ref[i // n_exp, i % n_exp]
        tile_end = jnp.sum(jnp.where(tile_eye, cum[:, LANES - 1:LANES], 0.0), axis=0, keepdims=True)
        tile_of_slot = jnp.sum(jnp.where(tile_end <= slot, 1.0, 0.0), axis=1, keepdims=True)
        pick_tile = jnp.where(tile_id == tile_of_slot, 1.0, 0.0).astype(BF16)
        cum_hi = jnp.floor(cum * (1.0 / 16.0))
        g1 = aff.astype(BF16)
        r1 = aff - g1.astype(F32)
        g2 = r1.astype(BF16)
        g3 = (r1 - g2.astype(F32)).astype(BF16)
        table = jnp.concatenate([cum_hi.astype(BF16), (cum - 16.0 * cum_hi).astype(BF16), g1, g2, g3], axis=1)
        got = jnp.dot(pick_tile, table, preferred_element_type=F32)
        counts = 16.0 * got[:, :LANES] + got[:, LANES:2 * LANES]
        within = jnp.sum(jnp.where(counts <= slot, 1.0, 0.0), axis=1, keepdims=True)
        affs = (got[:, 2 * LANES:3 * LANES] + got[:, 3 * LANES:4 * LANES]) + got[:, 4 * LANES:]
        gate = jnp.sum(jnp.where(lane_id == within, affs, 0.0), axis=1, keepdims=True)
        idx_ref[pl.ds(i, 1), :] = as_row(LANES * tile_of_slot + within).astype(jnp.int32)
        gate_ref[pl.ds(i, 1), :] = as_row(gate)
        return carry

    lax.fori_loop(0, lists // 2, lambda i, carry: per_list(2 * i + 1, per_list(2 * i, carry)), 0)


def _route(aff, cap):
    b, n_exp, nt, _ = aff.shape
    rows = n_exp * nt
    r = np.arange(rows)
    same_expert = (r[:, None] // nt) == (r[None, :] // nt)
    lane = np.arange(LANES)
    tri = jnp.asarray((lane[:, None] <= lane[None, :]).astype(np.float32), dtype=BF16)
    lower = jnp.asarray((same_expert & (r[None, :] < r[:, None])).astype(np.float32), dtype=BF16)
    whole = lambda i: (0, 0, 0, 0)
    return pl.pallas_call(
        functools.partial(_route_body, cap=cap),
        grid=(1,),
        in_specs=[pl.BlockSpec((b, n_exp, nt, LANES), whole),
                  pl.BlockSpec((LANES, LANES), lambda i: (0, 0)),
                  pl.BlockSpec((rows, rows), lambda i: (0, 0))],
        out_specs=[pl.BlockSpec((b * n_exp, cap), lambda i: (0, 0)),
                   pl.BlockSpec((b * n_exp, cap), lambda i: (0, 0))],
        out_shape=[jax.ShapeDtypeStruct((b * n_exp, cap), jnp.int32),
                   jax.ShapeDtypeStruct((b * n_exp, cap), F32)],
        scratch_shapes=[pltpu.VMEM((b * n_exp, nt, LANES), F32)],
        compiler_params=_params(("arbitrary",)),
        name="route",
    )(aff, tri, lower)


ROW_UNROLL = 8


def _gather_body(idx_ref, h_ref, o_ref, rows_ref, *, cap):
    base = (pl.program_id(0) * pl.num_programs(1) + pl.program_id(1)) * cap

    def body(i, carry):
        for r in range(ROW_UNROLL):
            rows_ref[i, pl.ds(r, 1), :] = h_ref[0, pl.ds(idx_ref[base + i * ROW_UNROLL + r], 1), :]
        return carry

    lax.fori_loop(0, cap // ROW_UNROLL, body, 0)
    o_ref[0] = rows_ref[...].reshape(cap, rows_ref.shape[-1]).astype(o_ref.dtype)


def _gather_rows(idx_flat, h, cap):
    b, t, d = h.shape
    return pl.pallas_call(
        functools.partial(_gather_body, cap=cap),
        grid_spec=pltpu.PrefetchScalarGridSpec(
            num_scalar_prefetch=1,
            grid=(b, N_EXPERTS),
            in_specs=[pl.BlockSpec((1, t, d), lambda i, e, idx: (i, 0, 0))],
            out_specs=pl.BlockSpec((1, cap, d), lambda i, e, idx: (e, i, 0)),
            scratch_shapes=[pltpu.VMEM((cap // ROW_UNROLL, ROW_UNROLL, d), F32)]),
        out_shape=jax.ShapeDtypeStruct((N_EXPERTS, b * cap, d), BF16),
        compiler_params=_params(("arbitrary", "arbitrary")),
        name="gather_rows",
    )(idx_flat, h)


def _combine_body(idx_ref, y_ref, o_ref, *, cap):
    e = pl.program_id(1)
    base = (pl.program_id(0) * pl.num_programs(1) + e) * cap

    @pl.when(e == 0)
    def _():
        o_ref[...] = jnp.zeros_like(o_ref)

    def body(i, carry):
        toks = [idx_ref[base + i * ROW_UNROLL + r] for r in range(ROW_UNROLL)]
        sums = [o_ref[0, pl.ds(toks[r], 1), :] + y_ref[0, i, pl.ds(r, 1), :] for r in range(ROW_UNROLL)]
        for r in range(ROW_UNROLL):
            o_ref[0, pl.ds(toks[r], 1), :] = sums[r]
        return carry

    lax.fori_loop(0, cap // ROW_UNROLL, body, 0)


def _combine_rows(idx_flat, y, b, t, cap):
    d = y.shape[-1]
    trips = cap // ROW_UNROLL
    y = y.reshape(N_EXPERTS, b * trips, ROW_UNROLL, d)
    return pl.pallas_call(
        functools.partial(_combine_body, cap=cap),
        grid_spec=pltpu.PrefetchScalarGridSpec(
            num_scalar_prefetch=1,
            grid=(b, N_EXPERTS),
            in_specs=[pl.BlockSpec((1, trips, ROW_UNROLL, d), lambda i, e, idx: (e, i, 0, 0))],
            out_specs=pl.BlockSpec((1, t, d), lambda i, e, idx: (i, 0, 0))),
        out_shape=jax.ShapeDtypeStruct((b, t, d), F32),
        compiler_params=_params(("arbitrary", "arbitrary")),
        name="combine_rows",
    )(idx_flat, y)


def _postnorm_body(x_ref, y_ref, g_ref, lng_ref, lnb_ref, o_ref):
    o_ref[0] = _layer_norm(DEEPNORM_ALPHA * x_ref[0] + g_ref[0] * y_ref[0]) * lng_ref[...] + lnb_ref[...]


def _post_norm(x, y, g, lng, lnb):
    b, t, d = x.shape
    tm = min(t, 1024)
    row = lambda i, j: (i, j, 0)
    return pl.pallas_call(
        _postnorm_body,
        grid=(b, t // tm),
        in_specs=[pl.BlockSpec((1, tm, d), row), pl.BlockSpec((1, tm, d), row),
                  pl.BlockSpec((1, 1, d), lambda i, j: (i, 0, 0)),
                  pl.BlockSpec((1, d), lambda i, j: (0, 0)), pl.BlockSpec((1, d), lambda i, j: (0, 0))],
        out_specs=pl.BlockSpec((1, tm, d), row),
        out_shape=jax.ShapeDtypeStruct((b, t, d), F32),
        compiler_params=_params(("arbitrary", "arbitrary")),
        name="post_norm",
    )(x, y, g, lng, lnb)


def _rope_tables(t):
    tok = jnp.arange(t, dtype=jnp.int32)
    lane = np.arange(LANES)
    d = lane % HEAD_DIM
    inv_freq = ROPE_BASE ** (-jnp.arange(0, HEAD_DIM // 2, 2, dtype=F32) / (HEAD_DIM // 2))
    freq = inv_freq[d % 16]
    use_col = jnp.asarray((d // 32) == 1)
    position = jnp.where(use_col[None, :], (tok % GRID_W)[:, None], (tok // GRID_W)[:, None])
    ang = position * freq[None, :]
    sign = jnp.asarray(np.where((d % 32) < 16, -1.0, 1.0).astype(np.float32))
    return jnp.cos(ang).astype(F32), (jnp.sin(ang) * sign[None, :]).astype(F32)


def _block_ones(width):
    idx = np.arange(width) // HEAD_DIM
    return jnp.asarray((idx[:, None] == idx[None, :]).astype(np.float32), dtype=BF16)


MIN_ROUTE_TILES = 16


def _moe(h2, aff, w_gate, w_up, w_down, layer):
    b, t, _ = h2.shape
    cap = EC_CAPACITY_FACTOR * t // N_EXPERTS
    nt = t // LANES
    aff_tiles = aff.reshape(b, N_EXPERTS, nt, LANES)
    if nt < MIN_ROUTE_TILES:
        aff_tiles = jnp.pad(aff_tiles, ((0, 0), (0, 0), (0, MIN_ROUTE_TILES - nt), (0, 0)))
    idx, gate = _route(aff_tiles, cap)
    idx_flat = idx.reshape(-1)
    xg = _gather_rows(idx_flat, h2, cap)
    gate_e = jnp.swapaxes(gate.reshape(b, N_EXPERTS, cap), 0, 1).reshape(N_EXPERTS, b * cap, 1)
    y = _expert_ffn(xg, w_gate, w_up, w_down, gate_e, layer)
    return _combine_rows(idx_flat, y, b, t, cap)


def kernel(x, c, ctx, c_ctx, w_mod, b_mod, w_in, q_norm_a, k_norm_a, w_fourier, b_fourier, sink_c, w_out, ln1_g,
           ln1_b, w_router, w_gate, w_up, w_down, ln2_g, ln2_b):
    b, t, d = x.shape
    cos, sin = _rope_tables(t)
    gq, gk = _block_ones(A_Q), _block_ones(A_KV)
    cc = jnp.zeros((MOD_ROWS, d), F32).at[:b].set(c).at[b].set(c_ctx)
    mod = _modulation(cc, w_mod, b_mod)
    w_in_bf = w_in.astype(BF16)
    w_out_bf = w_out.astype(BF16)

    x_lat, x_ctx = x, ctx
    for layer in range(DEPTH):
        update_ctx = layer < DEPTH - 1
        lat = [mod[layer, :b, i * d:(i + 1) * d][:, None, :] for i in range(6)]
        cm = [jnp.broadcast_to(mod[layer, b, i * d:(i + 1) * d][None, None, :], (b, 1, d)) for i in range(6)]
        qn = jnp.tile(q_norm_a[layer], A_HEADS)[None, :]
        kn = jnp.tile(k_norm_a[layer], A_KV_HEADS)[None, :]
        w_four = jax.scipy.linalg.block_diag(*[w_fourier[layer, g] for g in range(B_GROUPS)])
        bias = b_fourier[layer].reshape(1, B_W)
        lng1, lnb1 = ln1_g[layer][None, :], ln1_b[layer][None, :]
        lng2, lnb2 = ln2_g[layer][None, :], ln2_b[layer][None, :]
        wr_t = w_router[layer].T
        sink = sink_c[layer]

        proj_ctx = _in_projection(x_ctx, cm[0], cm[1], w_in_bf[layer], qn, kn, gq, gk, cos, sin, rope=False)
        qa_c, qc_c, u_c, ka_c, va_c, kc_c, vc_c = proj_ctx
        qa, qc, u, ka, va, kc, vc = _in_projection(x_lat, lat[0], lat[1], w_in_bf[layer], qn, kn, gq, gk, cos, sin,
                                                   rope=True)
        out_a = _attention(qa, ka_c, va_c, ka, va, heads_per_kv=4, mode="full")
        out_b = _fourier_mix(u, w_four, bias)
        out_c = _attention(qc, kc_c, vc_c, kc, vc, sink, heads_per_kv=2, mode="window")
        x1, h2, aff = _out_projection(out_a, out_b, out_c, w_out_bf[layer], x_lat, lat[2], lat[3], lat[4],
                                      lng1, lnb1, wr_t)
        y = _moe(h2, aff, w_gate, w_up, w_down, layer)
        x_lat = _post_norm(x1, y, lat[5], lng2, lnb2)

        if update_ctx:
            out_a_c = _attention(qa_c, ka_c, va_c, heads_per_kv=4, mode="none")
            out_b_c = _fourier_mix(u_c, w_four, bias)
            out_c_c = _attention(qc_c, kc_c, vc_c, sink=sink, heads_per_kv=2, mode="none")
            x1c, h2c, affc = _out_projection(out_a_c, out_b_c, out_c_c, w_out_bf[layer], x_ctx, cm[2], cm[3], cm[4],
                                             lng1, lnb1, wr_t)
            yc = _moe(h2c, affc, w_gate, w_up, w_down, layer)
            x_ctx = _post_norm(x1c, yc, cm[5], lng2, lnb2)
    return x_lat
```

```python
import functools

import numpy as np
import jax
import jax.numpy as jnp
from jax import lax
from jax.experimental import pallas as pl
from jax.experimental.pallas import tpu as pltpu

F32 = jnp.float32
BF16 = jnp.bfloat16

D_MODEL = 1024
DEPTH = 2
GRID_W = 64
HEAD_DIM = 64
ROPE_BASE = 10000.0
A_HEADS, A_KV_HEADS = 8, 2
B_GROUPS, B_GROUP_DIM = 4, 64
C_HEADS, C_KV_HEADS = 4, 2
WINDOW = 128
N_EXPERTS = 16
EC_CAPACITY_FACTOR = 2
EXPERT_FF = 2 * D_MODEL
A_Q, A_KV, B_W, C_Q, C_KV = 512, 128, 256, 256, 128
QU_WIDTH = A_Q + C_Q + B_W
IN_WIDTH = QU_WIDTH + 2 * A_KV + 2 * C_KV
LN_EPS = 1e-5
RMS_EPS = 1e-6
NEG_INF = -1e30
LOG2E = 1.4426950408889634
SAFE_LOGIT = 60.0
BOUND_SLACK = 1.02
DEEPNORM_ALPHA = (2 * DEPTH) ** 0.25
MOD_ROWS = 16
LANES = 128
ROW_SUBTILES = 4
WINDOW_Q_ROWS = 128
VMEM_LIMIT = 56 * 1024 * 1024

HIGHEST = lax.Precision.HIGHEST
NT_DIMS = (((1,), (1,)), ((), ()))


def _params(sem):
    return pltpu.CompilerParams(dimension_semantics=sem, vmem_limit_bytes=VMEM_LIMIT)


def _layer_norm(x):
    mu = jnp.mean(x, axis=-1, keepdims=True)
    xc = x - mu
    var = jnp.mean(xc * xc, axis=-1, keepdims=True)
    return xc * lax.rsqrt(var + LN_EPS)


def _silu(x):
    return x * (1.0 / (1.0 + jnp.exp(-x)))


def _mod_body(c_ref, w_ref, b_ref, o_ref):
    s = _silu(c_ref[...])
    o_ref[0] = jnp.dot(s, w_ref[0], precision=HIGHEST, preferred_element_type=F32) + b_ref[0]


def _modulation(cc, w_mod, b_mod):
    tn = 1024
    n = w_mod.shape[-1]
    return pl.pallas_call(
        _mod_body,
        grid=(DEPTH, n // tn),
        in_specs=[
            pl.BlockSpec((MOD_ROWS, D_MODEL), lambda l, j: (0, 0)),
            pl.BlockSpec((1, D_MODEL, tn), lambda l, j: (l, 0, j)),
            pl.BlockSpec((1, 1, tn), lambda l, j: (l, 0, j)),
        ],
        out_specs=pl.BlockSpec((1, MOD_ROWS, tn), lambda l, j: (l, 0, j)),
        out_shape=jax.ShapeDtypeStruct((DEPTH, MOD_ROWS, n), F32),
        compiler_params=_params(("arbitrary", "arbitrary")),
        name="modulation",
    )(cc, w_mod, b_mod.reshape(DEPTH, 1, n))


def _rope(x, cos, sin_signed, first_half):
    outs = []
    for j in range(x.shape[1] // LANES):
        xj = x[:, LANES * j:LANES * (j + 1)]
        partner = jnp.where(first_half, pltpu.roll(xj, LANES - 16, 1), pltpu.roll(xj, 16, 1))
        outs.append(xj * cos + partner * sin_signed)
    return jnp.concatenate(outs, axis=1) if len(outs) > 1 else outs[0]


def _dup_kv_heads(k, copies):
    lane = lax.broadcasted_iota(jnp.int32, (1, LANES), 1)
    lo = lane < HEAD_DIM
    r = pltpu.roll(k, HEAD_DIM, 1)
    d0 = jnp.where(lo, k, r)
    d1 = jnp.where(lo, r, k)
    reps = copies // 2
    return jnp.concatenate([d0] * reps + [d1] * reps, axis=1)


def _inproj_body(x_ref, sh_ref, sc_ref, w_ref, qn_ref, kn_ref, gq_ref, gk_ref, cos_ref, sin_ref,
                 qa_ref, qc_ref, u_ref, ka_ref, va_ref, kc_ref, vc_ref, *, rope):
    tm = x_ref.shape[1]
    sub = max(tm // ROW_SUBTILES, LANES)
    out_refs = (qa_ref, qc_ref, u_ref, ka_ref, va_ref, kc_ref, vc_ref)
    for i in range(tm // sub):
        rows = pl.ds(i * sub, sub)
        tables = (cos_ref[rows, :], sin_ref[rows, :]) if rope else None
        outs = _inproj_rows(x_ref[0, rows, :], sh_ref[0], sc_ref[0], w_ref, qn_ref[...], kn_ref[...], gq_ref, gk_ref,
                            tables)
        for ref, val in zip(out_refs, outs):
            ref[0, rows, :] = val


def _inproj_rows(x, sh, sc, w_ref, qn, kn, gq_ref, gk_ref, tables):
    h = _layer_norm(x) * (1.0 + sc) + sh
    p = jnp.dot(h.astype(BF16), w_ref[...], preferred_element_type=F32)
    qa = p[:, :A_Q]
    qc = p[:, A_Q:A_Q + C_Q]
    u = p[:, A_Q + C_Q:QU_WIDTH]
    ka = p[:, QU_WIDTH:QU_WIDTH + A_KV]
    va = p[:, QU_WIDTH + A_KV:QU_WIDTH + 2 * A_KV]
    kc = p[:, QU_WIDTH + 2 * A_KV:QU_WIDTH + 2 * A_KV + C_KV]
    vc = p[:, QU_WIDTH + 2 * A_KV + C_KV:]
    msq = jnp.dot((qa * qa).astype(BF16), gq_ref[...], preferred_element_type=F32) * (1.0 / HEAD_DIM)
    qa = qa * lax.rsqrt(msq + RMS_EPS) * qn
    msk = jnp.dot((ka * ka).astype(BF16), gk_ref[...], preferred_element_type=F32) * (1.0 / HEAD_DIM)
    ka = ka * lax.rsqrt(msk + RMS_EPS) * kn
    if tables is not None:
        cos, sin = tables
        lane = lax.broadcasted_iota(jnp.int32, (1, LANES), 1)
        first_half = (lane % 32) < 16
        qa = _rope(qa, cos, sin, first_half)
        qc = _rope(qc, cos, sin, first_half)
        ka = _rope(ka, cos, sin, first_half)
        kc = _rope(kc, cos, sin, first_half)
    scale = HEAD_DIM ** -0.5 * LOG2E
    return ((qa * scale).astype(BF16), (qc * scale).astype(BF16), u,
            _dup_kv_heads(ka, A_HEADS // A_KV_HEADS).astype(BF16),
            _dup_kv_heads(va, A_HEADS // A_KV_HEADS).astype(BF16),
            _dup_kv_heads(kc, C_HEADS // C_KV_HEADS).astype(BF16),
            _dup_kv_heads(vc, C_HEADS // C_KV_HEADS).astype(BF16))


def _in_projection(x, sh, sc, w_bf16, qn, kn, gq, gk, cos, sin, *, rope):
    b, t, d = x.shape
    tm = min(t, 512)
    row = lambda i, j: (i, j, 0)
    per_b = lambda i, j: (i, 0, 0)
    const = lambda i, j: (0, 0)
    tab = (lambda i, j: (j, 0)) if rope else const
    outs = [(A_Q, BF16), (C_Q, BF16), (B_W, F32), (4 * A_KV, BF16), (4 * A_KV, BF16),
            (2 * C_KV, BF16), (2 * C_KV, BF16)]
    return pl.pallas_call(
        functools.partial(_inproj_body, rope=rope),
        grid=(b, t // tm),
        in_specs=[
            pl.BlockSpec((1, tm, d), row),
            pl.BlockSpec((1, 1, d), per_b),
            pl.BlockSpec((1, 1, d), per_b),
            pl.BlockSpec((d, IN_WIDTH), const),
            pl.BlockSpec((1, A_Q), const),
            pl.BlockSpec((1, A_KV), const),
            pl.BlockSpec((A_Q, A_Q), const),
            pl.BlockSpec((A_KV, A_KV), const),
            pl.BlockSpec((tm, LANES), tab),
            pl.BlockSpec((tm, LANES), tab),
        ],
        out_specs=[pl.BlockSpec((1, tm, w), row) for w, _ in outs],
        out_shape=[jax.ShapeDtypeStruct((b, t, w), dt) for w, dt in outs],
        compiler_params=_params(("arbitrary", "arbitrary")),
        name="in_projection",
    )(x, sh, sc, w_bf16, qn, kn, gq, gk, cos, sin)


def _attn_body(*refs, heads_per_kv, mode, has_sink, tq, t_lat, kc_full):
    refs = list(refs)
    q_ref, qall_ref, kctx_ref, vctx_ref = refs[:4]
    pos = 4
    klat_ref = vlat_ref = sink_ref = None
    if mode != "none":
        klat_ref, vlat_ref = refs[pos:pos + 2]
        pos += 2
    if has_sink:
        sink_ref = refs[pos]
        pos += 1
    o_ref, bounded_ref = refs[pos:pos + 2]

    width = heads_per_kv * HEAD_DIM
    kvh = pl.program_id(1)
    qi = pl.program_id(2)
    q = q_ref[0]
    lane = lax.broadcasted_iota(jnp.int32, (1, width), 1)
    head_lanes = [(lane // HEAD_DIM) == g for g in range(heads_per_kv)]
    sinks = [sink_ref[kvh * heads_per_kv + g] * LOG2E for g in range(heads_per_kv)] if has_sink else None
    sub = min(tq, WINDOW_Q_ROWS) if mode == "window" else tq

    def per_head_lanes(cols):
        full = jnp.broadcast_to(cols[0], (sub, width))
        for g in range(1, heads_per_kv):
            full = jnp.where(head_lanes[g], cols[g], full)
        return full

    lane_col = lax.broadcasted_iota(jnp.int32, (width, 1), 0)
    head_ones = jnp.where((lane_col // HEAD_DIM) == (lane // HEAD_DIM), 1.0, 0.0).astype(BF16)

    def row_sq_norm_max(ref, per_head):
        rows = ref.shape[1]
        step_rows = min(rows, 1024)

        def body(c, best):
            xf = ref[0, pl.ds(pl.multiple_of(c * step_rows, step_rows), step_rows), :].astype(F32)
            xf = xf * xf
            if per_head:
                sq = jnp.max(jnp.dot(xf.astype(BF16), head_ones, preferred_element_type=F32), axis=1, keepdims=True)
            else:
                sq = jnp.sum(xf, axis=1, keepdims=True) * (1.0 / heads_per_kv)
            return jnp.maximum(best, jnp.max(sq, axis=0, keepdims=True))
        return lax.fori_loop(0, rows // step_rows, body, jnp.zeros((1, 1), F32))[0, 0]

    @pl.when(qi == 0)
    def _():
        k_sq = row_sq_norm_max(kctx_ref, False)
        if mode != "none":
            k_sq = jnp.maximum(k_sq, row_sq_norm_max(klat_ref, False))
        ok = row_sq_norm_max(qall_ref, True) * BOUND_SLACK * k_sq <= SAFE_LOGIT * SAFE_LOGIT
        if has_sink:
            for g in range(heads_per_kv):
                ok = jnp.logical_and(ok, jnp.abs(sinks[g]) <= SAFE_LOGIT)
        bounded_ref[0] = jnp.where(ok, 1, 0).astype(jnp.int32)

    def sweep(stabilised):
        groups = [sweep_rows(stabilised, r0) for r0 in range(0, tq, sub)]
        return jnp.concatenate(groups, axis=0) if len(groups) > 1 else groups[0]

    def sweep_rows(stabilised, r0):
        q_rows = q[r0:r0 + sub]
        q_stack = jnp.concatenate([jnp.where(hl, q_rows, jnp.zeros_like(q_rows)) for hl in head_lanes], axis=0)

        def by_head(stacked):
            return [stacked[g * sub:(g + 1) * sub] for g in range(heads_per_kv)]

        def step(carry, k, v, mask=None):
            m, l, acc = carry
            s = lax.dot_general(q_stack, k, NT_DIMS, preferred_element_type=F32)
            if mask is not None:
                s = jnp.where(jnp.concatenate([mask] * heads_per_kv, axis=0), s, NEG_INF)
            if stabilised:
                m_new = jnp.maximum(m, jnp.max(s, axis=1, keepdims=True))
                alpha = jnp.exp2(m - m_new)
                p = jnp.exp2(s - m_new)
                l = alpha * l + jnp.sum(p, axis=1, keepdims=True)
                acc = acc * per_head_lanes(by_head(alpha))
            else:
                m_new = m
                p = jnp.exp2(s)
                l = l + jnp.sum(p, axis=1, keepdims=True)
            pv = jnp.dot(p.astype(BF16), v, preferred_element_type=F32)
            for g, part in enumerate(by_head(pv)):
                acc = acc + jnp.where(head_lanes[g], part, 0.0)
            return m_new, l, acc

        rows_of = lambda vals: jnp.concatenate([jnp.full((sub, 1), v, F32) for v in vals], axis=0)
        if not has_sink:
            m0 = jnp.full((heads_per_kv * sub, 1), NEG_INF if stabilised else 0.0, F32)
            l0 = jnp.zeros((heads_per_kv * sub, 1), F32)
        elif stabilised:
            m0 = rows_of(sinks)
            l0 = jnp.ones((heads_per_kv * sub, 1), F32)
        else:
            m0 = jnp.zeros((heads_per_kv * sub, 1), F32)
            l0 = jnp.exp2(rows_of(sinks))
        carry = (m0, l0, jnp.zeros((sub, width), F32))
        carry = step(carry, kctx_ref[0], vctx_ref[0])
        if mode == "full":
            for c in range(t_lat // kc_full):
                rows = pl.ds(c * kc_full, kc_full)
                carry = step(carry, klat_ref[0, rows, :], vlat_ref[0, rows, :])
        elif mode == "window":
            span = sub + 2 * WINDOW
            q0 = qi * tq + r0
            k0 = pl.multiple_of(jnp.clip(q0 - WINDOW, 0, t_lat - span), WINDOW)
            qpos = q0 + lax.broadcasted_iota(jnp.int32, (sub, 1), 0)
            kpos = k0 + lax.broadcasted_iota(jnp.int32, (1, span), 1)
            in_window = jnp.abs(kpos - qpos) <= WINDOW
            carry = step(carry, klat_ref[0, pl.ds(k0, span), :], vlat_ref[0, pl.ds(k0, span), :], in_window)
        _, l, acc = carry
        return acc * per_head_lanes(by_head(1.0 / l))

    out = lax.cond(bounded_ref[0] == 1, lambda: sweep(False), lambda: sweep(True))
    o_ref[0] = out.astype(o_ref.dtype)


def _attention(q, kctx, vctx, klat=None, vlat=None, sink=None, *, heads_per_kv, mode):
    b, t, _ = q.shape
    n_kv = 2
    width = heads_per_kv * HEAD_DIM
    t_ctx = kctx.shape[1]
    tq = min(t, 512)
    qmap = lambda i, h, j: (i, j, h)
    kmap = lambda i, h, j: (i, 0, h)
    in_specs = [pl.BlockSpec((1, tq, width), qmap),
                pl.BlockSpec((1, t, width), kmap),
                pl.BlockSpec((1, t_ctx, width), kmap),
                pl.BlockSpec((1, t_ctx, width), kmap)]
    args = [q, q, kctx, vctx]
    t_lat = 0
    if mode != "none":
        t_lat = klat.shape[1]
        in_specs += [pl.BlockSpec((1, t_lat, width), kmap), pl.BlockSpec((1, t_lat, width), kmap)]
        args += [klat, vlat]
    if sink is not None:
        in_specs.append(pl.BlockSpec(memory_space=pltpu.SMEM))
        args.append(sink)
    body = functools.partial(_attn_body, heads_per_kv=heads_per_kv, mode=mode, has_sink=sink is not None,
                             tq=tq, t_lat=t_lat, kc_full=min(max(t_lat, 1), 1024))
    return pl.pallas_call(
        body,
        grid=(b, n_kv, t // tq),
        in_specs=in_specs,
        out_specs=pl.BlockSpec((1, tq, width), qmap),
        out_shape=jax.ShapeDtypeStruct((b, t, n_kv * width), BF16),
        scratch_shapes=[pltpu.SMEM((1,), jnp.int32)],
        compiler_params=_params(("arbitrary", "arbitrary", "arbitrary")),
        name=f"attention_{mode}_g{heads_per_kv}",
    )(*args)


def _split_bf16(x):
    hi = x.astype(BF16)
    return hi, (x - hi.astype(F32)).astype(BF16)


NN_DIMS = (((1,), (0,)), ((), ()))


def _dot3(a, b, dims=NN_DIMS):
    a_hi, a_lo = _split_bf16(a)
    b_hi, b_lo = _split_bf16(b)
    d = lambda x, y: lax.dot_general(x, y, dims, preferred_element_type=F32)
    return d(a_hi, b_hi) + (d(a_hi, b_lo) + d(a_lo, b_hi))


def _four_chan_body(u_ref, wblk_ref, cc_ref, sc_ref, z_ref):
    w = wblk_ref[...]
    wc = jnp.dot(cc_ref[...], w, precision=HIGHEST, preferred_element_type=F32)
    ws = jnp.dot(sc_ref[...], w, precision=HIGHEST, preferred_element_type=F32)
    u = u_ref[0]
    z_ref[0, 0] = _dot3(u, wc)
    z_ref[0, 1] = -_dot3(u, ws)


def _four_rows_body(z_ref, m_ref, tc_ref, ts_ref, a_ref):
    n1 = z_ref.shape[2]
    z = jnp.concatenate([z_ref[0, 0], z_ref[0, 1]], axis=0)
    a = _dot3(m_ref[...], z)
    ar, ai = a[:n1], a[n1:]
    tc, ts = tc_ref[...], ts_ref[...]
    a_ref[0, 0] = ar * tc + ai * ts
    a_ref[0, 1] = ai * tc - ar * ts


def _four_cols_body(a_ref, cs_ref, bias_ref, y_ref):
    kb = a_ref.shape[2]
    for j in range(kb):
        rhs = jnp.concatenate([a_ref[0, 0, j], a_ref[0, 1, j]], axis=0)
        y_ref[0, :, j, :] = _dot3(cs_ref[...], rhs) + bias_ref[...]


def _dft_constants(t):
    n = int(round(t ** 0.5))
    assert n * n == t
    k = np.arange(n)
    ang = 2.0 * np.pi * np.outer(k, k) / n
    c, s = np.cos(ang), np.sin(ang)
    rows = np.block([[c, s], [-s, c]]).astype(np.float32)
    cols = np.concatenate([c, s], axis=1).astype(np.float32)
    tw = 2.0 * np.pi * np.outer(k, k) / t
    tc = np.repeat(np.cos(tw), B_W, axis=1).astype(np.float32)
    ts = np.repeat(np.sin(tw), B_W, axis=1).astype(np.float32)
    return n, rows, cols, tc, ts


def _channel_constants(t):
    k = np.arange(B_GROUP_DIM)
    ang = 2.0 * np.pi * np.outer(k, k) / B_GROUP_DIM
    norm = 1.0 / np.sqrt(float(t) * B_GROUP_DIM)
    eye = np.eye(B_GROUPS)
    cc = np.kron(eye, np.cos(ang) * norm).astype(np.float32)
    sc = np.kron(eye, np.sin(ang) * norm).astype(np.float32)
    return cc, sc


def _fourier_mix(u, w_blockdiag, bias):
    b, t, _ = u.shape
    n, rows, cols, tc, ts = _dft_constants(t)
    cc, sc = _channel_constants(t)
    tm = min(t, 1024)
    z = pl.pallas_call(
        _four_chan_body,
        grid=(b, t // tm),
        in_specs=[pl.BlockSpec((1, tm, B_W), lambda i, j: (i, j, 0)),
                  pl.BlockSpec((B_W, B_W), lambda i, j: (0, 0)),
                  pl.BlockSpec((B_W, B_W), lambda i, j: (0, 0)),
                  pl.BlockSpec((B_W, B_W), lambda i, j: (0, 0))],
        out_specs=pl.BlockSpec((1, 2, tm, B_W), lambda i, j: (i, 0, j, 0)),
        out_shape=jax.ShapeDtypeStruct((b, 2, t, B_W), F32),
        compiler_params=_params(("arbitrary", "arbitrary")),
        name="fourier_channels",
    )(u, w_blockdiag, jnp.asarray(cc), jnp.asarray(sc))
    cols_total = n * B_W
    tc_cols = min(cols_total, 4096)
    a = pl.pallas_call(
        _four_rows_body,
        grid=(b, cols_total // tc_cols),
        in_specs=[pl.BlockSpec((1, 2, n, tc_cols), lambda i, j: (i, 0, 0, j)),
                  pl.BlockSpec((2 * n, 2 * n), lambda i, j: (0, 0)),
                  pl.BlockSpec((n, tc_cols), lambda i, j: (0, j)),
                  pl.BlockSpec((n, tc_cols), lambda i, j: (0, j))],
        out_specs=pl.BlockSpec((1, 2, n, tc_cols), lambda i, j: (i, 0, 0, j)),
        out_shape=jax.ShapeDtypeStruct((b, 2, n, cols_total), F32),
        compiler_params=_params(("arbitrary", "arbitrary")),
        name="fourier_rows",
    )(z.reshape(b, 2, n, cols_total), jnp.asarray(rows), jnp.asarray(tc), jnp.asarray(ts))
    kb = 8
    y = pl.pallas_call(
        _four_cols_body,
        grid=(b, n // kb),
        in_specs=[pl.BlockSpec((1, 2, kb, n, B_W), lambda i, j: (i, 0, j, 0, 0)),
                  pl.BlockSpec((n, 2 * n), lambda i, j: (0, 0)),
                  pl.BlockSpec((1, B_W), lambda i, j: (0, 0))],
        out_specs=pl.BlockSpec((1, n, kb, B_W), lambda i, j: (i, 0, j, 0)),
        out_shape=jax.ShapeDtypeStruct((b, n, n, B_W), F32),
        compiler_params=_params(("arbitrary", "arbitrary")),
        name="fourier_cols",
    )(a.reshape(b, 2, n, n, B_W), jnp.asarray(cols), bias)
    return y.reshape(b, t, B_W)


def _outproj_body(a_ref, f_ref, c_ref, w_ref, x_ref, g1_ref, sh2_ref, sc2_ref, lng_ref, lnb_ref, wr_ref,
                  x1_ref, h2_ref, aff_ref):
    tm = x_ref.shape[1]
    sub = max(tm // ROW_SUBTILES, LANES)
    for i in range(tm // sub):
        rows = pl.ds(i * sub, sub)
        o = jnp.dot(a_ref[0, rows, :], w_ref[:A_Q], preferred_element_type=F32)
        o = o + jnp.dot(f_ref[0, rows, :].astype(BF16), w_ref[A_Q:A_Q + B_W], preferred_element_type=F32)
        o = o + jnp.dot(c_ref[0, rows, :], w_ref[A_Q + B_W:], preferred_element_type=F32)
        x1 = _layer_norm(DEEPNORM_ALPHA * x_ref[0, rows, :] + g1_ref[0] * o) * lng_ref[...] + lnb_ref[...]
        x1_ref[0, rows, :] = x1
        h2 = _layer_norm(x1) * (1.0 + sc2_ref[0]) + sh2_ref[0]
        h2_ref[0, rows, :] = h2
        logits = _dot3(wr_ref[...], h2, NT_DIMS)
        e = jnp.exp(logits - jnp.max(logits, axis=0, keepdims=True))
        aff_ref[0, :, rows] = e / jnp.sum(e, axis=0, keepdims=True)


def _out_projection(oa, of, oc, w_bf16, x, g1, sh2, sc2, lng, lnb, wr_t):
    b, t, d = x.shape
    tm = min(t, 512)
    row = lambda i, j: (i, j, 0)
    per_b = lambda i, j: (i, 0, 0)
    const = lambda i, j: (0, 0)
    return pl.pallas_call(
        _outproj_body,
        grid=(b, t // tm),
        in_specs=[pl.BlockSpec((1, tm, A_Q), row),
                  pl.BlockSpec((1, tm, B_W), row),
                  pl.BlockSpec((1, tm, C_Q), row),
                  pl.BlockSpec((d, d), const),
                  pl.BlockSpec((1, tm, d), row),
                  pl.BlockSpec((1, 1, d), per_b),
                  pl.BlockSpec((1, 1, d), per_b),
                  pl.BlockSpec((1, 1, d), per_b),
                  pl.BlockSpec((1, d), const),
                  pl.BlockSpec((1, d), const),
                  pl.BlockSpec((N_EXPERTS, d), const)],
        out_specs=[pl.BlockSpec((1, tm, d), row),
                   pl.BlockSpec((1, tm, d), row),
                   pl.BlockSpec((1, N_EXPERTS, tm), lambda i, j: (i, 0, j))],
        out_shape=[jax.ShapeDtypeStruct((b, t, d), F32),
                   jax.ShapeDtypeStruct((b, t, d), F32),
                   jax.ShapeDtypeStruct((b, N_EXPERTS, t), F32)],
        compiler_params=_params(("arbitrary", "arbitrary")),
        name="out_projection",
    )(oa, of, oc, w_bf16, x, g1, sh2, sc2, lng, lnb, wr_t)


def _ffn_body(x_ref, wg_ref, wu_ref, wd_ref, gate_ref, o_ref):
    @pl.when(pl.program_id(2) == 0)
    def _():
        o_ref[...] = jnp.zeros_like(o_ref)

    x = x_ref[0]
    a = jnp.dot(x, wg_ref[0, 0].astype(BF16), preferred_element_type=F32)
    up = jnp.dot(x, wu_ref[0, 0].astype(BF16), preferred_element_type=F32)
    h = (_silu(a) * up).astype(BF16)
    y = jnp.dot(h, wd_ref[0, 0].astype(BF16), preferred_element_type=F32)
    o_ref[0] += gate_ref[0] * y


def _expert_ffn(xg, w_gate, w_up, w_down, gate, layer):
    e, m, d = xg.shape
    ff = w_gate.shape[-1]
    tm = min(m, 2048)
    tf = 512
    return pl.pallas_call(
        _ffn_body,
        grid=(e, m // tm, ff // tf),
        in_specs=[pl.BlockSpec((1, tm, d), lambda i, j, k: (i, j, 0)),
                  pl.BlockSpec((1, 1, d, tf), lambda i, j, k: (layer, i, 0, k)),
                  pl.BlockSpec((1, 1, d, tf), lambda i, j, k: (layer, i, 0, k)),
                  pl.BlockSpec((1, 1, tf, d), lambda i, j, k: (layer, i, k, 0)),
                  pl.BlockSpec((1, tm, 1), lambda i, j, k: (i, j, 0))],
        out_specs=pl.BlockSpec((1, tm, d), lambda i, j, k: (i, j, 0)),
        out_shape=jax.ShapeDtypeStruct((e, m, d), F32),
        compiler_params=_params(("arbitrary", "arbitrary", "arbitrary")),
        name="expert_ffn",
    )(xg, w_gate, w_up, w_down, gate)


def _route_body(aff_ref, tri_ref, lower_ref, idx_ref, gate_ref, cum_ref, *, cap):
    nb, n_exp, nt, _ = aff_ref.shape
    lists = nb * n_exp
    rows = n_exp * nt
    a = aff_ref[...].reshape(lists, nt, LANES)

    def count(mask):
        ones = jnp.where(mask, 1.0, 0.0)
        return jnp.sum(jnp.sum(ones, axis=1, keepdims=True), axis=2, keepdims=True)

    def token_prefix(mask):
        flat = jnp.where(mask, 1.0, 0.0).astype(BF16).reshape(nb * rows, LANES)
        in_row = jnp.dot(flat, tri_ref[...], preferred_element_type=F32)
        row_total = jnp.broadcast_to(in_row[:, LANES - 1:LANES], in_row.shape).astype(BF16)
        before = [jnp.dot(lower_ref[...], row_total[i * rows:(i + 1) * rows], preferred_element_type=F32)
                  for i in range(nb)]
        return (in_row + jnp.concatenate(before, axis=0)).reshape(lists, nt, LANES)

    def enough(cand):
        return count(a >= cand) >= cap

    power = jnp.ones((lists, 1, 1), F32)
    for k in range(6, -1, -1):
        too_few = jnp.logical_not(enough(power * 2.0 ** -(2 ** k - 1)))
        power = jnp.where(too_few, power * 2.0 ** -(2 ** k), power)
    base = jnp.where(enough(power), power, 0.0)

    def refine(_, carry):
        thr, digit = carry
        digit = digit * 0.5
        cand = thr + digit
        return jnp.where(enough(cand), cand, thr), digit

    thr, _ = lax.fori_loop(0, 23, refine, (base, base))
    above = a > thr
    tied = a == thr
    need = cap - count(above)
    chosen = above | (tied & (token_prefix(tied) <= need))
    cum_ref[...] = token_prefix(chosen)

    slot = lax.broadcasted_iota(jnp.int32, (cap, 1), 0).astype(F32)
    tile_eye = lax.broadcasted_iota(jnp.int32, (nt, nt), 0) == lax.broadcasted_iota(jnp.int32, (nt, nt), 1)
    tile_id = lax.broadcasted_iota(jnp.int32, (cap, nt), 1).astype(F32)
    lane_id = lax.broadcasted_iota(jnp.int32, (cap, LANES), 1).astype(F32)

    blk = min(cap, LANES)
    blk_eye = lax.broadcasted_iota(jnp.int32, (blk, blk), 0) == lax.broadcasted_iota(jnp.int32, (blk, blk), 1)

    def as_row(col):
        parts = [jnp.sum(jnp.where(blk_eye, col[j * blk:(j + 1) * blk], 0.0), axis=0, keepdims=True)
                 for j in range(cap // blk)]
        return jnp.concatenate(parts, axis=1) if len(parts) > 1 else parts[0]

    def per_list(i, carry):
        cum = cum_ref[i]
        aff = aff_ref[i // n_exp, i % n_exp]
        tile_end = jnp.sum(jnp.where(tile_eye, cum[:, LANES - 1:LANES], 0.0), axis=0, keepdims=True)
        tile_of_slot = jnp.sum(jnp.where(tile_end <= slot, 1.0, 0.0), axis=1, keepdims=True)
        pick_tile = jnp.where(tile_id == tile_of_slot, 1.0, 0.0).astype(BF16)
        cum_hi = jnp.floor(cum * (1.0 / 16.0))
        g1 = aff.astype(BF16)
        r1 = aff - g1.astype(F32)
        g2 = r1.astype(BF16)
        g3 = (r1 - g2.astype(F32)).astype(BF16)
        table = jnp.concatenate([cum_hi.astype(BF16), (cum - 16.0 * cum_hi).astype(BF16), g1, g2, g3], axis=1)
        got = jnp.dot(pick_tile, table, preferred_element_type=F32)
        counts = 16.0 * got[:, :LANES] + got[:, LANES:2 * LANES]
        within = jnp.sum(jnp.where(counts <= slot, 1.0, 0.0), axis=1, keepdims=True)
        affs = (got[:, 2 * LANES:3 * LANES] + got[:, 3 * LANES:4 * LANES]) + got[:, 4 * LANES:]
        gate = jnp.sum(jnp.where(lane_id == within, affs, 0.0), axis=1, keepdims=True)
        idx_ref[pl.ds(i, 1), :] = as_row(LANES * tile_of_slot + within).astype(jnp.int32)
        gate_ref[pl.ds(i, 1), :] = as_row(gate)
        return carry

    lax.fori_loop(0, lists // 2, lambda i, carry: per_list(2 * i + 1, per_list(2 * i, carry)), 0)


def _route(aff, cap):
    b, n_exp, nt, _ = aff.shape
    rows = n_exp * nt
    r = np.arange(rows)
    same_expert = (r[:, None] // nt) == (r[None, :] // nt)
    lane = np.arange(LANES)
    tri = jnp.asarray((lane[:, None] <= lane[None, :]).astype(np.float32), dtype=BF16)
    lower = jnp.asarray((same_expert & (r[None, :] < r[:, None])).astype(np.float32), dtype=BF16)
    whole = lambda i: (0, 0, 0, 0)
    return pl.pallas_call(
        functools.partial(_route_body, cap=cap),
        grid=(1,),
        in_specs=[pl.BlockSpec((b, n_exp, nt, LANES), whole),
                  pl.BlockSpec((LANES, LANES), lambda i: (0, 0)),
                  pl.BlockSpec((rows, rows), lambda i: (0, 0))],
        out_specs=[pl.BlockSpec((b * n_exp, cap), lambda i: (0, 0)),
                   pl.BlockSpec((b * n_exp, cap), lambda i: (0, 0))],
        out_shape=[jax.ShapeDtypeStruct((b * n_exp, cap), jnp.int32),
                   jax.ShapeDtypeStruct((b * n_exp, cap), F32)],
        scratch_shapes=[pltpu.VMEM((b * n_exp, nt, LANES), F32)],
        compiler_params=_params(("arbitrary",)),
        name="route",
    )(aff, tri, lower)


ROW_UNROLL = 8


def _gather_body(idx_ref, h_ref, o_ref, rows_ref, *, cap):
    base = (pl.program_id(0) * pl.num_programs(1) + pl.program_id(1)) * cap

    def body(i, carry):
        for r in range(ROW_UNROLL):
            rows_ref[i, pl.ds(r, 1), :] = h_ref[0, pl.ds(idx_ref[base + i * ROW_UNROLL + r], 1), :]
        return carry

    lax.fori_loop(0, cap // ROW_UNROLL, body, 0)
    o_ref[0] = rows_ref[...].reshape(cap, rows_ref.shape[-1]).astype(o_ref.dtype)


def _gather_rows(idx_flat, h, cap):
    b, t, d = h.shape
    return pl.pallas_call(
        functools.partial(_gather_body, cap=cap),
        grid_spec=pltpu.PrefetchScalarGridSpec(
            num_scalar_prefetch=1,
            grid=(b, N_EXPERTS),
            in_specs=[pl.BlockSpec((1, t, d), lambda i, e, idx: (i, 0, 0))],
            out_specs=pl.BlockSpec((1, cap, d), lambda i, e, idx: (e, i, 0)),
            scratch_shapes=[pltpu.VMEM((cap // ROW_UNROLL, ROW_UNROLL, d), F32)]),
        out_shape=jax.ShapeDtypeStruct((N_EXPERTS, b * cap, d), BF16),
        compiler_params=_params(("arbitrary", "arbitrary")),
        name="gather_rows",
    )(idx_flat, h)


def _combine_body(idx_ref, y_ref, o_ref, *, cap):
    e = pl.program_id(1)
    base = (pl.program_id(0) * pl.num_programs(1) + e) * cap

    @pl.when(e == 0)
    def _():
        o_ref[...] = jnp.zeros_like(o_ref)

    def body(i, carry):
        toks = [idx_ref[base + i * ROW_UNROLL + r] for r in range(ROW_UNROLL)]
        sums = [o_ref[0, pl.ds(toks[r], 1), :] + y_ref[0, i, pl.ds(r, 1), :] for r in range(ROW_UNROLL)]
        for r in range(ROW_UNROLL):
            o_ref[0, pl.ds(toks[r], 1), :] = sums[r]
        return carry

    lax.fori_loop(0, cap // ROW_UNROLL, body, 0)


def _combine_rows(idx_flat, y, b, t, cap):
    d = y.shape[-1]
    trips = cap // ROW_UNROLL
    y = y.reshape(N_EXPERTS, b * trips, ROW_UNROLL, d)
    return pl.pallas_call(
        functools.partial(_combine_body, cap=cap),
        grid_spec=pltpu.PrefetchScalarGridSpec(
            num_scalar_prefetch=1,
            grid=(b, N_EXPERTS),
            in_specs=[pl.BlockSpec((1, trips, ROW_UNROLL, d), lambda i, e, idx: (e, i, 0, 0))],
            out_specs=pl.BlockSpec((1, t, d), lambda i, e, idx: (i, 0, 0))),
        out_shape=jax.ShapeDtypeStruct((b, t, d), F32),
        compiler_params=_params(("arbitrary", "arbitrary")),
        name="combine_rows",
    )(idx_flat, y)


def _postnorm_body(x_ref, y_ref, g_ref, lng_ref, lnb_ref, o_ref):
    o_ref[0] = _layer_norm(DEEPNORM_ALPHA * x_ref[0] + g_ref[0] * y_ref[0]) * lng_ref[...] + lnb_ref[...]


def _post_norm(x, y, g, lng, lnb):
    b, t, d = x.shape
    tm = min(t, 1024)
    row = lambda i, j: (i, j, 0)
    return pl.pallas_call(
        _postnorm_body,
        grid=(b, t // tm),
        in_specs=[pl.BlockSpec((1, tm, d), row), pl.BlockSpec((1, tm, d), row),
                  pl.BlockSpec((1, 1, d), lambda i, j: (i, 0, 0)),
                  pl.BlockSpec((1, d), lambda i, j: (0, 0)), pl.BlockSpec((1, d), lambda i, j: (0, 0))],
        out_specs=pl.BlockSpec((1, tm, d), row),
        out_shape=jax.ShapeDtypeStruct((b, t, d), F32),
        compiler_params=_params(("arbitrary", "arbitrary")),
        name="post_norm",
    )(x, y, g, lng, lnb)


def _rope_tables(t):
    tok = jnp.arange(t, dtype=jnp.int32)
    lane = np.arange(LANES)
    d = lane % HEAD_DIM
    inv_freq = ROPE_BASE ** (-jnp.arange(0, HEAD_DIM // 2, 2, dtype=F32) / (HEAD_DIM // 2))
    freq = inv_freq[d % 16]
    use_col = jnp.asarray((d // 32) == 1)
    position = jnp.where(use_col[None, :], (tok % GRID_W)[:, None], (tok // GRID_W)[:, None])
    ang = position * freq[None, :]
    sign = jnp.asarray(np.where((d % 32) < 16, -1.0, 1.0).astype(np.float32))
    return jnp.cos(ang).astype(F32), (jnp.sin(ang) * sign[None, :]).astype(F32)


def _block_ones(width):
    idx = np.arange(width) // HEAD_DIM
    return jnp.asarray((idx[:, None] == idx[None, :]).astype(np.float32), dtype=BF16)


MIN_ROUTE_TILES = 16


def _moe(h2, aff, w_gate, w_up, w_down, layer):
    b, t, _ = h2.shape
    cap = EC_CAPACITY_FACTOR * t // N_EXPERTS
    nt = t // LANES
    aff_tiles = aff.reshape(b, N_EXPERTS, nt, LANES)
    if nt < MIN_ROUTE_TILES:
        aff_tiles = jnp.pad(aff_tiles, ((0, 0), (0, 0), (0, MIN_ROUTE_TILES - nt), (0, 0)))
    idx, gate = _route(aff_tiles, cap)
    idx_flat = idx.reshape(-1)
    xg = _gather_rows(idx_flat, h2, cap)
    gate_e = jnp.swapaxes(gate.reshape(b, N_EXPERTS, cap), 0, 1).reshape(N_EXPERTS, b * cap, 1)
    y = _expert_ffn(xg, w_gate, w_up, w_down, gate_e, layer)
    return _combine_rows(idx_flat, y, b, t, cap)


def kernel(x, c, ctx, c_ctx, w_mod, b_mod, w_in, q_norm_a, k_norm_a, w_fourier, b_fourier, sink_c, w_out, ln1_g,
           ln1_b, w_router, w_gate, w_up, w_down, ln2_g, ln2_b):
    b, t, d = x.shape
    cos, sin = _rope_tables(t)
    gq, gk = _block_ones(A_Q), _block_ones(A_KV)
    cc = jnp.zeros((MOD_ROWS, d), F32).at[:b].set(c).at[b].set(c_ctx)
    mod = _modulation(cc, w_mod, b_mod)
    w_in_bf = w_in.astype(BF16)
    w_out_bf = w_out.astype(BF16)

    x_lat, x_ctx = x, ctx
    for layer in range(DEPTH):
        update_ctx = layer < DEPTH - 1
        lat = [mod[layer, :b, i * d:(i + 1) * d][:, None, :] for i in range(6)]
        cm = [jnp.broadcast_to(mod[layer, b, i * d:(i + 1) * d][None, None, :], (b, 1, d)) for i in range(6)]
        qn = jnp.tile(q_norm_a[layer], A_HEADS)[None, :]
        kn = jnp.tile(k_norm_a[layer], A_KV_HEADS)[None, :]
        w_four = jax.scipy.linalg.block_diag(*[w_fourier[layer, g] for g in range(B_GROUPS)])
        bias = b_fourier[layer].reshape(1, B_W)
        lng1, lnb1 = ln1_g[layer][None, :], ln1_b[layer][None, :]
        lng2, lnb2 = ln2_g[layer][None, :], ln2_b[layer][None, :]
        wr_t = w_router[layer].T
        sink = sink_c[layer]

        proj_ctx = _in_projection(x_ctx, cm[0], cm[1], w_in_bf[layer], qn, kn, gq, gk, cos, sin, rope=False)
        qa_c, qc_c, u_c, ka_c, va_c, kc_c, vc_c = proj_ctx
        qa, qc, u, ka, va, kc, vc = _in_projection(x_lat, lat[0], lat[1], w_in_bf[layer], qn, kn, gq, gk, cos, sin,
                                                   rope=True)
        out_a = _attention(qa, ka_c, va_c, ka, va, heads_per_kv=4, mode="full")
        out_b = _fourier_mix(u, w_four, bias)
        out_c = _attention(qc, kc_c, vc_c, kc, vc, sink, heads_per_kv=2, mode="window")
        x1, h2, aff = _out_projection(out_a, out_b, out_c, w_out_bf[layer], x_lat, lat[2], lat[3], lat[4],
                                      lng1, lnb1, wr_t)
        y = _moe(h2, aff, w_gate, w_up, w_down, layer)
        x_lat = _post_norm(x1, y, lat[5], lng2, lnb2)

        if update_ctx:
            out_a_c = _attention(qa_c, ka_c, va_c, heads_per_kv=4, mode="none")
            out_b_c = _fourier_mix(u_c, w_four, bias)
            out_c_c = _attention(qc_c, kc_c, vc_c, sink=sink, heads_per_kv=2, mode="none")
            x1c, h2c, affc = _out_projection(out_a_c, out_b_c, out_c_c, w_out_bf[layer], x_ctx, cm[2], cm[3], cm[4],
                                             lng1, lnb1, wr_t)
            yc = _moe(h2c, affc, w_gate, w_up, w_down, layer)
            x_ctx = _post_norm(x1c, yc, cm[5], lng2, lnb2)
    return x_lat
```

```python
import functools

import numpy as np
import jax
import jax.numpy as jnp
from jax import lax
from jax.experimental import pallas as pl
from jax.experimental.pallas import tpu as pltpu

F32 = jnp.float32
BF16 = jnp.bfloat16

D_MODEL = 1024
DEPTH = 2
GRID_W = 64
HEAD_DIM = 64
ROPE_BASE = 10000.0
A_HEADS, A_KV_HEADS = 8, 2
B_GROUPS, B_GROUP_DIM = 4, 64
C_HEADS, C_KV_HEADS = 4, 2
WINDOW = 128
N_EXPERTS = 16
EC_CAPACITY_FACTOR = 2
EXPERT_FF = 2 * D_MODEL
A_Q, A_KV, B_W, C_Q, C_KV = 512, 128, 256, 256, 128
QU_WIDTH = A_Q + C_Q + B_W
IN_WIDTH = QU_WIDTH + 2 * A_KV + 2 * C_KV
LN_EPS = 1e-5
RMS_EPS = 1e-6
NEG_INF = -1e30
LOG2E = 1.4426950408889634
SAFE_LOGIT = 60.0
BOUND_SLACK = 1.02
DEEPNORM_ALPHA = (2 * DEPTH) ** 0.25
MOD_ROWS = 16
LANES = 128
ROW_SUBTILES = 4
WINDOW_Q_ROWS = 128
VMEM_LIMIT = 56 * 1024 * 1024

HIGHEST = lax.Precision.HIGHEST
NT_DIMS = (((1,), (1,)), ((), ()))


def _params(sem):
    return pltpu.CompilerParams(dimension_semantics=sem, vmem_limit_bytes=VMEM_LIMIT)


def _layer_norm(x):
    mu = jnp.mean(x, axis=-1, keepdims=True)
    xc = x - mu
    var = jnp.mean(xc * xc, axis=-1, keepdims=True)
    return xc * lax.rsqrt(var + LN_EPS)


def _silu(x):
    return x * (1.0 / (1.0 + jnp.exp(-x)))


def _mod_body(c_ref, w_ref, b_ref, o_ref):
    s = _silu(c_ref[...])
    o_ref[0] = jnp.dot(s, w_ref[0], precision=HIGHEST, preferred_element_type=F32) + b_ref[0]


def _modulation(cc, w_mod, b_mod):
    tn = 1024
    n = w_mod.shape[-1]
    return pl.pallas_call(
        _mod_body,
        grid=(DEPTH, n // tn),
        in_specs=[
            pl.BlockSpec((MOD_ROWS, D_MODEL), lambda l, j: (0, 0)),
            pl.BlockSpec((1, D_MODEL, tn), lambda l, j: (l, 0, j)),
            pl.BlockSpec((1, 1, tn), lambda l, j: (l, 0, j)),
        ],
        out_specs=pl.BlockSpec((1, MOD_ROWS, tn), lambda l, j: (l, 0, j)),
        out_shape=jax.ShapeDtypeStruct((DEPTH, MOD_ROWS, n), F32),
        compiler_params=_params(("arbitrary", "arbitrary")),
        name="modulation",
    )(cc, w_mod, b_mod.reshape(DEPTH, 1, n))


def _rope(x, cos, sin_signed, first_half):
    outs = []
    for j in range(x.shape[1] // LANES):
        xj = x[:, LANES * j:LANES * (j + 1)]
        partner = jnp.where(first_half, pltpu.roll(xj, LANES - 16, 1), pltpu.roll(xj, 16, 1))
        outs.append(xj * cos + partner * sin_signed)
    return jnp.concatenate(outs, axis=1) if len(outs) > 1 else outs[0]


def _dup_kv_heads(k, copies):
    lane = lax.broadcasted_iota(jnp.int32, (1, LANES), 1)
    lo = lane < HEAD_DIM
    r = pltpu.roll(k, HEAD_DIM, 1)
    d0 = jnp.where(lo, k, r)
    d1 = jnp.where(lo, r, k)
    reps = copies // 2
    return jnp.concatenate([d0] * reps + [d1] * reps, axis=1)


def _inproj_body(x_ref, sh_ref, sc_ref, w_ref, qn_ref, kn_ref, gq_ref, gk_ref, cos_ref, sin_ref,
                 qa_ref, qc_ref, u_ref, ka_ref, va_ref, kc_ref, vc_ref, *, rope):
    tm = x_ref.shape[1]
    sub = max(tm // ROW_SUBTILES, LANES)
    out_refs = (qa_ref, qc_ref, u_ref, ka_ref, va_ref, kc_ref, vc_ref)
    for i in range(tm // sub):
        rows = pl.ds(i * sub, sub)
        tables = (cos_ref[rows, :], sin_ref[rows, :]) if rope else None
        outs = _inproj_rows(x_ref[0, rows, :], sh_ref[0], sc_ref[0], w_ref, qn_ref[...], kn_ref[...], gq_ref, gk_ref,
                            tables)
        for ref, val in zip(out_refs, outs):
            ref[0, rows, :] = val


def _inproj_rows(x, sh, sc, w_ref, qn, kn, gq_ref, gk_ref, tables):
    h = _layer_norm(x) * (1.0 + sc) + sh
    p = jnp.dot(h.astype(BF16), w_ref[...], preferred_element_type=F32)
    qa = p[:, :A_Q]
    qc = p[:, A_Q:A_Q + C_Q]
    u = p[:, A_Q + C_Q:QU_WIDTH]
    ka = p[:, QU_WIDTH:QU_WIDTH + A_KV]
    va = p[:, QU_WIDTH + A_KV:QU_WIDTH + 2 * A_KV]
    kc = p[:, QU_WIDTH + 2 * A_KV:QU_WIDTH + 2 * A_KV + C_KV]
    vc = p[:, QU_WIDTH + 2 * A_KV + C_KV:]
    msq = jnp.dot((qa * qa).astype(BF16), gq_ref[...], preferred_element_type=F32) * (1.0 / HEAD_DIM)
    qa = qa * lax.rsqrt(msq + RMS_EPS) * qn
    msk = jnp.dot((ka * ka).astype(BF16), gk_ref[...], preferred_element_type=F32) * (1.0 / HEAD_DIM)
    ka = ka * lax.rsqrt(msk + RMS_EPS) * kn
    if tables is not None:
        cos, sin = tables
        lane = lax.broadcasted_iota(jnp.int32, (1, LANES), 1)
        first_half = (lane % 32) < 16
        qa = _rope(qa, cos, sin, first_half)
        qc = _rope(qc, cos, sin, first_half)
        ka = _rope(ka, cos, sin, first_half)
        kc = _rope(kc, cos, sin, first_half)
    scale = HEAD_DIM ** -0.5 * LOG2E
    return ((qa * scale).astype(BF16), (qc * scale).astype(BF16), u,
            _dup_kv_heads(ka, A_HEADS // A_KV_HEADS).astype(BF16),
            _dup_kv_heads(va, A_HEADS // A_KV_HEADS).astype(BF16),
            _dup_kv_heads(kc, C_HEADS // C_KV_HEADS).astype(BF16),
            _dup_kv_heads(vc, C_HEADS // C_KV_HEADS).astype(BF16))


def _in_projection(x, sh, sc, w_bf16, qn, kn, gq, gk, cos, sin, *, rope):
    b, t, d = x.shape
    tm = min(t, 512)
    row = lambda i, j: (i, j, 0)
    per_b = lambda i, j: (i, 0, 0)
    const = lambda i, j: (0, 0)
    tab = (lambda i, j: (j, 0)) if rope else const
    outs = [(A_Q, BF16), (C_Q, BF16), (B_W, F32), (4 * A_KV, BF16), (4 * A_KV, BF16),
            (2 * C_KV, BF16), (2 * C_KV, BF16)]
    return pl.pallas_call(
        functools.partial(_inproj_body, rope=rope),
        grid=(b, t // tm),
        in_specs=[
            pl.BlockSpec((1, tm, d), row),
            pl.BlockSpec((1, 1, d), per_b),
            pl.BlockSpec((1, 1, d), per_b),
            pl.BlockSpec((d, IN_WIDTH), const),
            pl.BlockSpec((1, A_Q), const),
            pl.BlockSpec((1, A_KV), const),
            pl.BlockSpec((A_Q, A_Q), const),
            pl.BlockSpec((A_KV, A_KV), const),
            pl.BlockSpec((tm, LANES), tab),
            pl.BlockSpec((tm, LANES), tab),
        ],
        out_specs=[pl.BlockSpec((1, tm, w), row) for w, _ in outs],
        out_shape=[jax.ShapeDtypeStruct((b, t, w), dt) for w, dt in outs],
        compiler_params=_params(("arbitrary", "arbitrary")),
        name="in_projection",
    )(x, sh, sc, w_bf16, qn, kn, gq, gk, cos, sin)


def _attn_body(*refs, heads_per_kv, mode, has_sink, tq, t_lat, kc_full):
    refs = list(refs)
    q_ref, qall_ref, kctx_ref, vctx_ref = refs[:4]
    pos = 4
    klat_ref = vlat_ref = sink_ref = None
    if mode != "none":
        klat_ref, vlat_ref = refs[pos:pos + 2]
        pos += 2
    if has_sink:
        sink_ref = refs[pos]
        pos += 1
    o_ref, bounded_ref = refs[pos:pos + 2]

    width = heads_per_kv * HEAD_DIM
    kvh = pl.program_id(1)
    qi = pl.program_id(2)
    q = q_ref[0]
    lane = lax.broadcasted_iota(jnp.int32, (1, width), 1)
    head_lanes = [(lane // HEAD_DIM) == g for g in range(heads_per_kv)]
    sinks = [sink_ref[kvh * heads_per_kv + g] * LOG2E for g in range(heads_per_kv)] if has_sink else None
    sub = min(tq, WINDOW_Q_ROWS) if mode == "window" else tq

    def per_head_lanes(cols):
        full = jnp.broadcast_to(cols[0], (sub, width))
        for g in range(1, heads_per_kv):
            full = jnp.where(head_lanes[g], cols[g], full)
        return full

    lane_col = lax.broadcasted_iota(jnp.int32, (width, 1), 0)
    head_ones = jnp.where((lane_col // HEAD_DIM) == (lane // HEAD_DIM), 1.0, 0.0).astype(BF16)

    def row_sq_norm_max(ref, per_head):
        rows = ref.shape[1]
        step_rows = min(rows, 1024)

        def body(c, best):
            xf = ref[0, pl.ds(pl.multiple_of(c * step_rows, step_rows), step_rows), :].astype(F32)
            xf = xf * xf
            if per_head:
                sq = jnp.max(jnp.dot(xf.astype(BF16), head_ones, preferred_element_type=F32), axis=1, keepdims=True)
            else:
                sq = jnp.sum(xf, axis=1, keepdims=True) * (1.0 / heads_per_kv)
            return jnp.maximum(best, jnp.max(sq, axis=0, keepdims=True))
        return lax.fori_loop(0, rows // step_rows, body, jnp.zeros((1, 1), F32))[0, 0]

    @pl.when(qi == 0)
    def _():
        k_sq = row_sq_norm_max(kctx_ref, False)
        if mode != "none":
            k_sq = jnp.maximum(k_sq, row_sq_norm_max(klat_ref, False))
        ok = row_sq_norm_max(qall_ref, True) * BOUND_SLACK * k_sq <= SAFE_LOGIT * SAFE_LOGIT
        if has_sink:
            for g in range(heads_per_kv):
                ok = jnp.logical_and(ok, jnp.abs(sinks[g]) <= SAFE_LOGIT)
        bounded_ref[0] = jnp.where(ok, 1, 0).astype(jnp.int32)

    def sweep(stabilised):
        groups = [sweep_rows(stabilised, r0) for r0 in range(0, tq, sub)]
        return jnp.concatenate(groups, axis=0) if len(groups) > 1 else groups[0]

    def sweep_rows(stabilised, r0):
        q_rows = q[r0:r0 + sub]
        q_stack = jnp.concatenate([jnp.where(hl, q_rows, jnp.zeros_like(q_rows)) for hl in head_lanes], axis=0)

        def by_head(stacked):
            return [stacked[g * sub:(g + 1) * sub] for g in range(heads_per_kv)]

        def step(carry, k, v, mask=None):
            m, l, acc = carry
            s = lax.dot_general(q_stack, k, NT_DIMS, preferred_element_type=F32)
            if mask is not None:
                s = jnp.where(jnp.concatenate([mask] * heads_per_kv, axis=0), s, NEG_INF)
            if stabilised:
                m_new = jnp.maximum(m, jnp.max(s, axis=1, keepdims=True))
                alpha = jnp.exp2(m - m_new)
                p = jnp.exp2(s - m_new)
                l = alpha * l + jnp.sum(p, axis=1, keepdims=True)
                acc = acc * per_head_lanes(by_head(alpha))
            else:
                m_new = m
                p = jnp.exp2(s)
                l = l + jnp.sum(p, axis=1, keepdims=True)
            pv = jnp.dot(p.astype(BF16), v, preferred_element_type=F32)
            for g, part in enumerate(by_head(pv)):
                acc = acc + jnp.where(head_lanes[g], part, 0.0)
            return m_new, l, acc

        rows_of = lambda vals: jnp.concatenate([jnp.full((sub, 1), v, F32) for v in vals], axis=0)
        if not has_sink:
            m0 = jnp.full((heads_per_kv * sub, 1), NEG_INF if stabilised else 0.0, F32)
            l0 = jnp.zeros((heads_per_kv * sub, 1), F32)
        elif stabilised:
            m0 = rows_of(sinks)
            l0 = jnp.ones((heads_per_kv * sub, 1), F32)
        else:
            m0 = jnp.zeros((heads_per_kv * sub, 1), F32)
            l0 = jnp.exp2(rows_of(sinks))
        carry = (m0, l0, jnp.zeros((sub, width), F32))
        carry = step(carry, kctx_ref[0], vctx_ref[0])
        if mode == "full":
            for c in range(t_lat // kc_full):
                rows = pl.ds(c * kc_full, kc_full)
                carry = step(carry, klat_ref[0, rows, :], vlat_ref[0, rows, :])
        elif mode == "window":
            span = sub + 2 * WINDOW
            q0 = qi * tq + r0
            k0 = pl.multiple_of(jnp.clip(q0 - WINDOW, 0, t_lat - span), WINDOW)
            qpos = q0 + lax.broadcasted_iota(jnp.int32, (sub, 1), 0)
            kpos = k0 + lax.broadcasted_iota(jnp.int32, (1, span), 1)
            in_window = jnp.abs(kpos - qpos) <= WINDOW
            carry = step(carry, klat_ref[0, pl.ds(k0, span), :], vlat_ref[0, pl.ds(k0, span), :], in_window)
        _, l, acc = carry
        return acc * per_head_lanes(by_head(1.0 / l))

    out = lax.cond(bounded_ref[0] == 1, lambda: sweep(False), lambda: sweep(True))
    o_ref[0] = out.astype(o_ref.dtype)


def _attention(q, kctx, vctx, klat=None, vlat=None, sink=None, *, heads_per_kv, mode):
    b, t, _ = q.shape
    n_kv = 2
    width = heads_per_kv * HEAD_DIM
    t_ctx = kctx.shape[1]
    tq = min(t, 512)
    qmap = lambda i, h, j: (i, j, h)
    kmap = lambda i, h, j: (i, 0, h)
    in_specs = [pl.BlockSpec((1, tq, width), qmap),
                pl.BlockSpec((1, t, width), kmap),
                pl.BlockSpec((1, t_ctx, width), kmap),
                pl.BlockSpec((1, t_ctx, width), kmap)]
    args = [q, q, kctx, vctx]
    t_lat = 0
    if mode != "none":
        t_lat = klat.shape[1]
        in_specs += [pl.BlockSpec((1, t_lat, width), kmap), pl.BlockSpec((1, t_lat, width), kmap)]
        args += [klat, vlat]
    if sink is not None:
        in_specs.append(pl.BlockSpec(memory_space=pltpu.SMEM))
        args.append(sink)
    body = functools.partial(_attn_body, heads_per_kv=heads_per_kv, mode=mode, has_sink=sink is not None,
                             tq=tq, t_lat=t_lat, kc_full=min(max(t_lat, 1), 1024))
    return pl.pallas_call(
        body,
        grid=(b, n_kv, t // tq),
        in_specs=in_specs,
        out_specs=pl.BlockSpec((1, tq, width), qmap),
        out_shape=jax.ShapeDtypeStruct((b, t, n_kv * width), BF16),
        scratch_shapes=[pltpu.SMEM((1,), jnp.int32)],
        compiler_params=_params(("arbitrary", "arbitrary", "arbitrary")),
        name=f"attention_{mode}_g{heads_per_kv}",
    )(*args)


def _split_bf16(x):
    hi = x.astype(BF16)
    return hi, (x - hi.astype(F32)).astype(BF16)


NN_DIMS = (((1,), (0,)), ((), ()))


def _dot3(a, b, dims=NN_DIMS):
    a_hi, a_lo = _split_bf16(a)
    b_hi, b_lo = _split_bf16(b)
    d = lambda x, y: lax.dot_general(x, y, dims, preferred_element_type=F32)
    return d(a_hi, b_hi) + (d(a_hi, b_lo) + d(a_lo, b_hi))


def _four_chan_body(u_ref, wblk_ref, cc_ref, sc_ref, z_ref):
    w = wblk_ref[...]
    wc = jnp.dot(cc_ref[...], w, precision=HIGHEST, preferred_element_type=F32)
    ws = jnp.dot(sc_ref[...], w, precision=HIGHEST, preferred_element_type=F32)
    u = u_ref[0]
    z_ref[0, 0] = _dot3(u, wc)
    z_ref[0, 1] = -_dot3(u, ws)


def _four_rows_body(z_ref, m_ref, tc_ref, ts_ref, a_ref):
    n1 = z_ref.shape[2]
    z = jnp.concatenate([z_ref[0, 0], z_ref[0, 1]], axis=0)
    a = _dot3(m_ref[...], z)
    ar, ai = a[:n1], a[n1:]
    tc, ts = tc_ref[...], ts_ref[...]
    a_ref[0, 0] = ar * tc + ai * ts
    a_ref[0, 1] = ai * tc - ar * ts


def _four_cols_body(a_ref, cs_ref, bias_ref, y_ref):
    kb = a_ref.shape[2]
    for j in range(kb):
        rhs = jnp.concatenate([a_ref[0, 0, j], a_ref[0, 1, j]], axis=0)
        y_ref[0, :, j, :] = _dot3(cs_ref[...], rhs) + bias_ref[...]


def _dft_constants(t):
    n = int(round(t ** 0.5))
    assert n * n == t
    k = np.arange(n)
    ang = 2.0 * np.pi * np.outer(k, k) / n
    c, s = np.cos(ang), np.sin(ang)
    rows = np.block([[c, s], [-s, c]]).astype(np.float32)
    cols = np.concatenate([c, s], axis=1).astype(np.float32)
    tw = 2.0 * np.pi * np.outer(k, k) / t
    tc = np.repeat(np.cos(tw), B_W, axis=1).astype(np.float32)
    ts = np.repeat(np.sin(tw), B_W, axis=1).astype(np.float32)
    return n, rows, cols, tc, ts


def _channel_constants(t):
    k = np.arange(B_GROUP_DIM)
    ang = 2.0 * np.pi * np.outer(k, k) / B_GROUP_DIM
    norm = 1.0 / np.sqrt(float(t) * B_GROUP_DIM)
    eye = np.eye(B_GROUPS)
    cc = np.kron(eye, np.cos(ang) * norm).astype(np.float32)
    sc = np.kron(eye, np.sin(ang) * norm).astype(np.float32)
    return cc, sc


def _fourier_mix(u, w_blockdiag, bias):
    b, t, _ = u.shape
    n, rows, cols, tc, ts = _dft_constants(t)
    cc, sc = _channel_constants(t)
    tm = min(t, 1024)
    z = pl.pallas_call(
        _four_chan_body,
        grid=(b, t // tm),
        in_specs=[pl.BlockSpec((1, tm, B_W), lambda i, j: (i, j, 0)),
                  pl.BlockSpec((B_W, B_W), lambda i, j: (0, 0)),
                  pl.BlockSpec((B_W, B_W), lambda i, j: (0, 0)),
                  pl.BlockSpec((B_W, B_W), lambda i, j: (0, 0))],
        out_specs=pl.BlockSpec((1, 2, tm, B_W), lambda i, j: (i, 0, j, 0)),
        out_shape=jax.ShapeDtypeStruct((b, 2, t, B_W), F32),
        compiler_params=_params(("arbitrary", "arbitrary")),
        name="fourier_channels",
    )(u, w_blockdiag, jnp.asarray(cc), jnp.asarray(sc))
    cols_total = n * B_W
    tc_cols = min(cols_total, 4096)
    a = pl.pallas_call(
        _four_rows_body,
        grid=(b, cols_total // tc_cols),
        in_specs=[pl.BlockSpec((1, 2, n, tc_cols), lambda i, j: (i, 0, 0, j)),
                  pl.BlockSpec((2 * n, 2 * n), lambda i, j: (0, 0)),
                  pl.BlockSpec((n, tc_cols), lambda i, j: (0, j)),
                  pl.BlockSpec((n, tc_cols), lambda i, j: (0, j))],
        out_specs=pl.BlockSpec((1, 2, n, tc_cols), lambda i, j: (i, 0, 0, j)),
        out_shape=jax.ShapeDtypeStruct((b, 2, n, cols_total), F32),
        compiler_params=_params(("arbitrary", "arbitrary")),
        name="fourier_rows",
    )(z.reshape(b, 2, n, cols_total), jnp.asarray(rows), jnp.asarray(tc), jnp.asarray(ts))
    kb = 8
    y = pl.pallas_call(
        _four_cols_body,
        grid=(b, n // kb),
        in_specs=[pl.BlockSpec((1, 2, kb, n, B_W), lambda i, j: (i, 0, j, 0, 0)),
                  pl.BlockSpec((n, 2 * n), lambda i, j: (0, 0)),
                  pl.BlockSpec((1, B_W), lambda i, j: (0, 0))],
        out_specs=pl.BlockSpec((1, n, kb, B_W), lambda i, j: (i, 0, j, 0)),
        out_shape=jax.ShapeDtypeStruct((b, n, n, B_W), F32),
        compiler_params=_params(("arbitrary", "arbitrary")),
        name="fourier_cols",
    )(a.reshape(b, 2, n, n, B_W), jnp.asarray(cols), bias)
    return y.reshape(b, t, B_W)


def _outproj_body(a_ref, f_ref, c_ref, w_ref, x_ref, g1_ref, sh2_ref, sc2_ref, lng_ref, lnb_ref, wr_ref,
                  x1_ref, h2_ref, aff_ref):
    tm = x_ref.shape[1]
    sub = max(tm // ROW_SUBTILES, LANES)
    for i in range(tm // sub):
        rows = pl.ds(i * sub, sub)
        o = jnp.dot(a_ref[0, rows, :], w_ref[:A_Q], preferred_element_type=F32)
        o = o + jnp.dot(f_ref[0, rows, :].astype(BF16), w_ref[A_Q:A_Q + B_W], preferred_element_type=F32)
        o = o + jnp.dot(c_ref[0, rows, :], w_ref[A_Q + B_W:], preferred_element_type=F32)
        x1 = _layer_norm(DEEPNORM_ALPHA * x_ref[0, rows, :] + g1_ref[0] * o) * lng_ref[...] + lnb_ref[...]
        x1_ref[0, rows, :] = x1
        h2 = _layer_norm(x1) * (1.0 + sc2_ref[0]) + sh2_ref[0]
        h2_ref[0, rows, :] = h2
        logits = _dot3(wr_ref[...], h2, NT_DIMS)
        e = jnp.exp(logits - jnp.max(logits, axis=0, keepdims=True))
        aff_ref[0, :, rows] = e / jnp.sum(e, axis=0, keepdims=True)


def _out_projection(oa, of, oc, w_bf16, x, g1, sh2, sc2, lng, lnb, wr_t):
    b, t, d = x.shape
    tm = min(t, 512)
    row = lambda i, j: (i, j, 0)
    per_b = lambda i, j: (i, 0, 0)
    const = lambda i, j: (0, 0)
    return pl.pallas_call(
        _outproj_body,
        grid=(b, t // tm),
        in_specs=[pl.BlockSpec((1, tm, A_Q), row),
                  pl.BlockSpec((1, tm, B_W), row),
                  pl.BlockSpec((1, tm, C_Q), row),
                  pl.BlockSpec((d, d), const),
                  pl.BlockSpec((1, tm, d), row),
                  pl.BlockSpec((1, 1, d), per_b),
                  pl.BlockSpec((1, 1, d), per_b),
                  pl.BlockSpec((1, 1, d), per_b),
                  pl.BlockSpec((1, d), const),
                  pl.BlockSpec((1, d), const),
                  pl.BlockSpec((N_EXPERTS, d), const)],
        out_specs=[pl.BlockSpec((1, tm, d), row),
                   pl.BlockSpec((1, tm, d), row),
                   pl.BlockSpec((1, N_EXPERTS, tm), lambda i, j: (i, 0, j))],
        out_shape=[jax.ShapeDtypeStruct((b, t, d), F32),
                   jax.ShapeDtypeStruct((b, t, d), F32),
                   jax.ShapeDtypeStruct((b, N_EXPERTS, t), F32)],
        compiler_params=_params(("arbitrary", "arbitrary")),
        name="out_projection",
    )(oa, of, oc, w_bf16, x, g1, sh2, sc2, lng, lnb, wr_t)


def _ffn_body(x_ref, wg_ref, wu_ref, wd_ref, gate_ref, o_ref):
    @pl.when(pl.program_id(2) == 0)
    def _():
        o_ref[...] = jnp.zeros_like(o_ref)

    x = x_ref[0]
    a = jnp.dot(x, wg_ref[0, 0].astype(BF16), preferred_element_type=F32)
    up = jnp.dot(x, wu_ref[0, 0].astype(BF16), preferred_element_type=F32)
    h = (_silu(a) * up).astype(BF16)
    y = jnp.dot(h, wd_ref[0, 0].astype(BF16), preferred_element_type=F32)
    o_ref[0] += gate_ref[0] * y


def _expert_ffn(xg, w_gate, w_up, w_down, gate, layer):
    e, m, d = xg.shape
    ff = w_gate.shape[-1]
    tm = min(m, 2048)
    tf = 512
    return pl.pallas_call(
        _ffn_body,
        grid=(e, m // tm, ff // tf),
        in_specs=[pl.BlockSpec((1, tm, d), lambda i, j, k: (i, j, 0)),
                  pl.BlockSpec((1, 1, d, tf), lambda i, j, k: (layer, i, 0, k)),
                  pl.BlockSpec((1, 1, d, tf), lambda i, j, k: (layer, i, 0, k)),
                  pl.BlockSpec((1, 1, tf, d), lambda i, j, k: (layer, i, k, 0)),
                  pl.BlockSpec((1, tm, 1), lambda i, j, k: (i, j, 0))],
        out_specs=pl.BlockSpec((1, tm, d), lambda i, j, k: (i, j, 0)),
        out_shape=jax.ShapeDtypeStruct((e, m, d), F32),
        compiler_params=_params(("arbitrary", "arbitrary", "arbitrary")),
        name="expert_ffn",
    )(xg, w_gate, w_up, w_down, gate)


def _route_body(aff_ref, tri_ref, lower_ref, idx_ref, gate_ref, cum_ref, *, cap):
    nb, n_exp, nt, _ = aff_ref.shape
    lists = nb * n_exp
    rows = n_exp * nt
    a = aff_ref[...].reshape(lists, nt, LANES)

    def count(mask):
        ones = jnp.where(mask, 1.0, 0.0)
        return jnp.sum(jnp.sum(ones, axis=1, keepdims=True), axis=2, keepdims=True)

    def token_prefix(mask):
        flat = jnp.where(mask, 1.0, 0.0).astype(BF16).reshape(nb * rows, LANES)
        in_row = jnp.dot(flat, tri_ref[...], preferred_element_type=F32)
        row_total = jnp.broadcast_to(in_row[:, LANES - 1:LANES], in_row.shape).astype(BF16)
        before = [jnp.dot(lower_ref[...], row_total[i * rows:(i + 1) * rows], preferred_element_type=F32)
                  for i in range(nb)]
        return (in_row + jnp.concatenate(before, axis=0)).reshape(lists, nt, LANES)

    def enough(cand):
        return count(a >= cand) >= cap

    power = jnp.ones((lists, 1, 1), F32)
    for k in range(6, -1, -1):
        too_few = jnp.logical_not(enough(power * 2.0 ** -(2 ** k - 1)))
        power = jnp.where(too_few, power * 2.0 ** -(2 ** k), power)
    base = jnp.where(enough(power), power, 0.0)

    def refine(_, carry):
        thr, digit = carry
        digit = digit * 0.5
        cand = thr + digit
        return jnp.where(enough(cand), cand, thr), digit

    thr, _ = lax.fori_loop(0, 23, refine, (base, base))
    above = a > thr
    tied = a == thr
    need = cap - count(above)
    chosen = above | (tied & (token_prefix(tied) <= need))
    cum_ref[...] = token_prefix(chosen)

    slot = lax.broadcasted_iota(jnp.int32, (cap, 1), 0).astype(F32)
    tile_eye = lax.broadcasted_iota(jnp.int32, (nt, nt), 0) == lax.broadcasted_iota(jnp.int32, (nt, nt), 1)
    tile_id = lax.broadcasted_iota(jnp.int32, (cap, nt), 1).astype(F32)
    lane_id = lax.broadcasted_iota(jnp.int32, (cap, LANES), 1).astype(F32)

    blk = min(cap, LANES)
    blk_eye = lax.broadcasted_iota(jnp.int32, (blk, blk), 0) == lax.broadcasted_iota(jnp.int32, (blk, blk), 1)

    def as_row(col):
        parts = [jnp.sum(jnp.where(blk_eye, col[j * blk:(j + 1) * blk], 0.0), axis=0, keepdims=True)
                 for j in range(cap // blk)]
        return jnp.concatenate(parts, axis=1) if len(parts) > 1 else parts[0]

    def per_list(i, carry):
        cum = cum_ref[i]
        aff = aff_ref[i // n_exp, i % n_exp]
        tile_end = jnp.sum(jnp.where(tile_eye, cum[:, LANES - 1:LANES], 0.0), axis=0, keepdims=True)
        tile_of_slot = jnp.sum(jnp.where(tile_end <= slot, 1.0, 0.0), axis=1, keepdims=True)
        pick_tile = jnp.where(tile_id == tile_of_slot, 1.0, 0.0).astype(BF16)
        cum_hi = jnp.floor(cum * (1.0 / 16.0))
        g1 = aff.astype(BF16)
        r1 = aff - g1.astype(F32)
        g2 = r1.astype(BF16)
        g3 = (r1 - g2.astype(F32)).astype(BF16)
        table = jnp.concatenate([cum_hi.astype(BF16), (cum - 16.0 * cum_hi).astype(BF16), g1, g2, g3], axis=1)
        got = jnp.dot(pick_tile, table, preferred_element_type=F32)
        counts = 16.0 * got[:, :LANES] + got[:, LANES:2 * LANES]
        within = jnp.sum(jnp.where(counts <= slot, 1.0, 0.0), axis=1, keepdims=True)
        affs = (got[:, 2 * LANES:3 * LANES] + got[:, 3 * LANES:4 * LANES]) + got[:, 4 * LANES:]
        gate = jnp.sum(jnp.where(lane_id == within, affs, 0.0), axis=1, keepdims=True)
        idx_ref[pl.ds(i, 1), :] = as_row(LANES * tile_of_slot + within).astype(jnp.int32)
        gate_ref[pl.ds(i, 1), :] = as_row(gate)
        return carry

    lax.fori_loop(0, lists // 2, lambda i, carry: per_list(2 * i + 1, per_list(2 * i, carry)), 0)


def _route(aff, cap):
    b, n_exp, nt, _ = aff.shape
    rows = n_exp * nt
    r = np.arange(rows)
    same_expert = (r[:, None] // nt) == (r[None, :] // nt)
    lane = np.arange(LANES)
    tri = jnp.asarray((lane[:, None] <= lane[None, :]).astype(np.float32), dtype=BF16)
    lower = jnp.asarray((same_expert & (r[None, :] < r[:, None])).astype(np.float32), dtype=BF16)
    whole = lambda i: (0, 0, 0, 0)
    return pl.pallas_call(
        functools.partial(_route_body, cap=cap),
        grid=(1,),
        in_specs=[pl.BlockSpec((b, n_exp, nt, LANES), whole),
                  pl.BlockSpec((LANES, LANES), lambda i: (0, 0)),
                  pl.BlockSpec((rows, rows), lambda i: (0, 0))],
        out_specs=[pl.BlockSpec((b * n_exp, cap), lambda i: (0, 0)),
                   pl.BlockSpec((b * n_exp, cap), lambda i: (0, 0))],
        out_shape=[jax.ShapeDtypeStruct((b * n_exp, cap), jnp.int32),
                   jax.ShapeDtypeStruct((b * n_exp, cap), F32)],
        scratch_shapes=[pltpu.VMEM((b * n_exp, nt, LANES), F32)],
        compiler_params=_params(("arbitrary",)),
        name="route",
    )(aff, tri, lower)


ROW_UNROLL = 16


def _gather_body(idx_ref, h_ref, o_ref, rows_ref, *, cap):
    base = (pl.program_id(0) * pl.num_programs(1) + pl.program_id(1)) * cap

    def body(i, carry):
        for r in range(ROW_UNROLL):
            rows_ref[i, pl.ds(r, 1), :] = h_ref[0, pl.ds(idx_ref[base + i * ROW_UNROLL + r], 1), :]
        return carry

    lax.fori_loop(0, cap // ROW_UNROLL, body, 0)
    o_ref[0] = rows_ref[...].reshape(cap, rows_ref.shape[-1]).astype(o_ref.dtype)


def _gather_rows(idx_flat, h, cap):
    b, t, d = h.shape
    return pl.pallas_call(
        functools.partial(_gather_body, cap=cap),
        grid_spec=pltpu.PrefetchScalarGridSpec(
            num_scalar_prefetch=1,
            grid=(b, N_EXPERTS),
            in_specs=[pl.BlockSpec((1, t, d), lambda i, e, idx: (i, 0, 0))],
            out_specs=pl.BlockSpec((1, cap, d), lambda i, e, idx: (e, i, 0)),
            scratch_shapes=[pltpu.VMEM((cap // ROW_UNROLL, ROW_UNROLL, d), F32)]),
        out_shape=jax.ShapeDtypeStruct((N_EXPERTS, b * cap, d), BF16),
        compiler_params=_params(("arbitrary", "arbitrary")),
        name="gather_rows",
    )(idx_flat, h)


def _combine_body(idx_ref, y_ref, o_ref, *, cap):
    e = pl.program_id(1)
    base = (pl.program_id(0) * pl.num_programs(1) + e) * cap

    @pl.when(e == 0)
    def _():
        o_ref[...] = jnp.zeros_like(o_ref)

    def body(i, carry):
        toks = [idx_ref[base + i * ROW_UNROLL + r] for r in range(ROW_UNROLL)]
        sums = [o_ref[0, pl.ds(toks[r], 1), :] + y_ref[0, i, pl.ds(r, 1), :] for r in range(ROW_UNROLL)]
        for r in range(ROW_UNROLL):
            o_ref[0, pl.ds(toks[r], 1), :] = sums[r]
        return carry

    lax.fori_loop(0, cap // ROW_UNROLL, body, 0)


def _combine_rows(idx_flat, y, b, t, cap):
    d = y.shape[-1]
    trips = cap // ROW_UNROLL
    y = y.reshape(N_EXPERTS, b * trips, ROW_UNROLL, d)
    return pl.pallas_call(
        functools.partial(_combine_body, cap=cap),
        grid_spec=pltpu.PrefetchScalarGridSpec(
            num_scalar_prefetch=1,
            grid=(b, N_EXPERTS),
            in_specs=[pl.BlockSpec((1, trips, ROW_UNROLL, d), lambda i, e, idx: (e, i, 0, 0))],
            out_specs=pl.BlockSpec((1, t, d), lambda i, e, idx: (i, 0, 0))),
        out_shape=jax.ShapeDtypeStruct((b, t, d), F32),
        compiler_params=_params(("arbitrary", "arbitrary")),
        name="combine_rows",
    )(idx_flat, y)


def _postnorm_body(x_ref, y_ref, g_ref, lng_ref, lnb_ref, o_ref):
    o_ref[0] = _layer_norm(DEEPNORM_ALPHA * x_ref[0] + g_ref[0] * y_ref[0]) * lng_ref[...] + lnb_ref[...]


def _post_norm(x, y, g, lng, lnb):
    b, t, d = x.shape
    tm = min(t, 1024)
    row = lambda i, j: (i, j, 0)
    return pl.pallas_call(
        _postnorm_body,
        grid=(b, t // tm),
        in_specs=[pl.BlockSpec((1, tm, d), row), pl.BlockSpec((1, tm, d), row),
                  pl.BlockSpec((1, 1, d), lambda i, j: (i, 0, 0)),
                  pl.BlockSpec((1, d), lambda i, j: (0, 0)), pl.BlockSpec((1, d), lambda i, j: (0, 0))],
        out_specs=pl.BlockSpec((1, tm, d), row),
        out_shape=jax.ShapeDtypeStruct((b, t, d), F32),
        compiler_params=_params(("arbitrary", "arbitrary")),
        name="post_norm",
    )(x, y, g, lng, lnb)


def _rope_tables(t):
    tok = jnp.arange(t, dtype=jnp.int32)
    lane = np.arange(LANES)
    d = lane % HEAD_DIM
    inv_freq = ROPE_BASE ** (-jnp.arange(0, HEAD_DIM // 2, 2, dtype=F32) / (HEAD_DIM // 2))
    freq = inv_freq[d % 16]
    use_col = jnp.asarray((d // 32) == 1)
    position = jnp.where(use_col[None, :], (tok % GRID_W)[:, None], (tok // GRID_W)[:, None])
    ang = position * freq[None, :]
    sign = jnp.asarray(np.where((d % 32) < 16, -1.0, 1.0).astype(np.float32))
    return jnp.cos(ang).astype(F32), (jnp.sin(ang) * sign[None, :]).astype(F32)


def _block_ones(width):
    idx = np.arange(width) // HEAD_DIM
    return jnp.asarray((idx[:, None] == idx[None, :]).astype(np.float32), dtype=BF16)


MIN_ROUTE_TILES = 16


def _moe(h2, aff, w_gate, w_up, w_down, layer):
    b, t, _ = h2.shape
    cap = EC_CAPACITY_FACTOR * t // N_EXPERTS
    nt = t // LANES
    aff_tiles = aff.reshape(b, N_EXPERTS, nt, LANES)
    if nt < MIN_ROUTE_TILES:
        aff_tiles = jnp.pad(aff_tiles, ((0, 0), (0, 0), (0, MIN_ROUTE_TILES - nt), (0, 0)))
    idx, gate = _route(aff_tiles, cap)
    idx_flat = idx.reshape(-1)
    xg = _gather_rows(idx_flat, h2, cap)
    gate_e = jnp.swapaxes(gate.reshape(b, N_EXPERTS, cap), 0, 1).reshape(N_EXPERTS, b * cap, 1)
    y = _expert_ffn(xg, w_gate, w_up, w_down, gate_e, layer)
    return _combine_rows(idx_flat, y, b, t, cap)


def kernel(x, c, ctx, c_ctx, w_mod, b_mod, w_in, q_norm_a, k_norm_a, w_fourier, b_fourier, sink_c, w_out, ln1_g,
           ln1_b, w_router, w_gate, w_up, w_down, ln2_g, ln2_b):
    b, t, d = x.shape
    cos, sin = _rope_tables(t)
    gq, gk = _block_ones(A_Q), _block_ones(A_KV)
    cc = jnp.zeros((MOD_ROWS, d), F32).at[:b].set(c).at[b].set(c_ctx)
    mod = _modulation(cc, w_mod, b_mod)
    w_in_bf = w_in.astype(BF16)
    w_out_bf = w_out.astype(BF16)

    x_lat, x_ctx = x, ctx
    for layer in range(DEPTH):
        update_ctx = layer < DEPTH - 1
        lat = [mod[layer, :b, i * d:(i + 1) * d][:, None, :] for i in range(6)]
        cm = [jnp.broadcast_to(mod[layer, b, i * d:(i + 1) * d][None, None, :], (b, 1, d)) for i in range(6)]
        qn = jnp.tile(q_norm_a[layer], A_HEADS)[None, :]
        kn = jnp.tile(k_norm_a[layer], A_KV_HEADS)[None, :]
        w_four = jax.scipy.linalg.block_diag(*[w_fourier[layer, g] for g in range(B_GROUPS)])
        bias = b_fourier[layer].reshape(1, B_W)
        lng1, lnb1 = ln1_g[layer][None, :], ln1_b[layer][None, :]
        lng2, lnb2 = ln2_g[layer][None, :], ln2_b[layer][None, :]
        wr_t = w_router[layer].T
        sink = sink_c[layer]

        proj_ctx = _in_projection(x_ctx, cm[0], cm[1], w_in_bf[layer], qn, kn, gq, gk, cos, sin, rope=False)
        qa_c, qc_c, u_c, ka_c, va_c, kc_c, vc_c = proj_ctx
        qa, qc, u, ka, va, kc, vc = _in_projection(x_lat, lat[0], lat[1], w_in_bf[layer], qn, kn, gq, gk, cos, sin,
                                                   rope=True)
        out_a = _attention(qa, ka_c, va_c, ka, va, heads_per_kv=4, mode="full")
        out_b = _fourier_mix(u, w_four, bias)
        out_c = _attention(qc, kc_c, vc_c, kc, vc, sink, heads_per_kv=2, mode="window")
        x1, h2, aff = _out_projection(out_a, out_b, out_c, w_out_bf[layer], x_lat, lat[2], lat[3], lat[4],
                                      lng1, lnb1, wr_t)
        y = _moe(h2, aff, w_gate, w_up, w_down, layer)
        x_lat = _post_norm(x1, y, lat[5], lng2, lnb2)

        if update_ctx:
            out_a_c = _attention(qa_c, ka_c, va_c, heads_per_kv=4, mode="none")
            out_b_c = _fourier_mix(u_c, w_four, bias)
            out_c_c = _attention(qc_c, kc_c, vc_c, sink=sink, heads_per_kv=2, mode="none")
            x1c, h2c, affc = _out_projection(out_a_c, out_b_c, out_c_c, w_out_bf[layer], x_ctx, cm[2], cm[3], cm[4],
                                             lng1, lnb1, wr_t)
            yc = _moe(h2c, affc, w_gate, w_up, w_down, layer)
            x_ctx = _post_norm(x1c, yc, cm[5], lng2, lnb2)
    return x_lat
```
